```python
import functools
import jax, jax.numpy as jnp
from jax import lax
import numpy as np

D_MODEL = 2048
BATCH = 4
SEQ = 2048
DEPTH = 4
DEC_BATCH = 8
DEC_SEQ = 4
PAST_LEN = 16384
PAGE_SIZE = 128

HEAD_DIM = 64
N_HEADS = D_MODEL // 128
ATTN_DIM = N_HEADS * HEAD_DIM
POOL_WINDOWS = (2, 4, 8, 16)
N_POOL_GROUPS = len(POOL_WINDOWS)
POOL_DIM = D_MODEL // 4
POOL_GROUP = POOL_DIM // N_POOL_GROUPS
POOL_OUT_GROUP = D_MODEL // N_POOL_GROUPS
POOL_STATE = max(POOL_WINDOWS) - 1
CONV_DIM = D_MODEL // 4
CONV_WIDTH = 31
CONV_STATE = CONV_WIDTH - 1
N_BRANCHES = 3
FFN_DIM = 7 * D_MODEL // 2
N_EXPERTS = 8
TOP_K = 2
PLE_DIM = 256
Q_BLOCK = 128
N_DENSE = (DEPTH + 1) // 2
N_MOE = DEPTH // 2
RMS_EPS = 1e-6
LN_EPS = 1e-5

C_POOL = 0
C_Q = C_POOL + POOL_DIM
C_K = C_Q + ATTN_DIM
C_V = C_K + ATTN_DIM
C_F = C_V + ATTN_DIM
C_GLU = C_F + N_HEADS
C_GATE = C_GLU + 2 * CONV_DIM
IN_COLS = C_GATE + N_BRANCHES * D_MODEL

kernel_name = 'hybrid_pool_fox_conformer_moe_step'


def rms_norm(x, g):
    x32 = x.astype(jnp.float32)
    y = x32 * lax.rsqrt(jnp.mean(x32 * x32, axis=-1, keepdims=True) + RMS_EPS)
    return (y * g.astype(jnp.float32)).astype(x.dtype)


def layer_norm(x, g, b):
    x32 = x.astype(jnp.float32)
    mu = jnp.mean(x32, axis=-1, keepdims=True)
    xc = x32 - mu
    y = xc * lax.rsqrt(jnp.mean(xc * xc, axis=-1, keepdims=True) + LN_EPS)
    return (y * g.astype(jnp.float32) + b.astype(jnp.float32)).astype(x.dtype)


def pool_mixer(u, prefix, offset, w_pool, pool_scale):
    B, L, _ = u.shape
    ext = jnp.concatenate([prefix.astype(u.dtype), u], axis=1)
    cs = jnp.cumsum(ext.astype(jnp.float32), axis=1)
    cs = jnp.concatenate([jnp.zeros((B, 1, POOL_DIM), jnp.float32), cs], axis=1)
    hi = cs[:, POOL_STATE + 1:]
    pos = (offset + jnp.arange(L)).astype(jnp.float32)
    u32 = u.astype(jnp.float32)
    outs = []
    for g, w in enumerate(POOL_WINDOWS):
        sl = slice(g * POOL_GROUP, (g + 1) * POOL_GROUP)
        lo = cs[:, POOL_STATE + 1 - w:POOL_STATE + 1 - w + L, sl]
        cnt = jnp.minimum(pos + 1.0, float(w))[None, :, None]
        outs.append((hi[..., sl] - lo) / cnt - u32[..., sl])
    r = jnp.stack(outs, axis=2).astype(u.dtype)
    y = jnp.einsum('blgc,gcd->blgd', r, w_pool).reshape(B, L, D_MODEL)
    return y * pool_scale, ext[:, -POOL_STATE:]


def conv_module(glu_in, prefix, conv_w, conv_b, ln_g, ln_b, w_conv_out):
    a, b = jnp.split(glu_in, 2, axis=-1)
    u = a * jax.nn.sigmoid(b)
    ext = jnp.concatenate([prefix.astype(u.dtype), u], axis=1)
    y = lax.conv_general_dilated(ext, conv_w[:, None, :].astype(ext.dtype), window_strides=(1,),
                                 padding='VALID', dimension_numbers=('NWC', 'WIO', 'NWC'),
                                 feature_group_count=CONV_DIM) + conv_b
    y = jax.nn.silu(layer_norm(y, ln_g, ln_b))
    return y @ w_conv_out, ext[:, -CONV_STATE:]


def fox_attention(q, k, v, cq, ck, q_pos, k_pos):
    B, Lq, H, Dh = q.shape
    qb = Q_BLOCK if Lq % Q_BLOCK == 0 else Lq
    nb = Lq // qb
    scale = HEAD_DIM ** -0.5
    qs = q.reshape(B, nb, qb, H, Dh).transpose(1, 0, 2, 3, 4)
    cqs = cq.reshape(B, nb, qb, H).transpose(1, 0, 3, 2)
    ps = q_pos.reshape(nb, qb)
    ckT = jnp.transpose(ck, (0, 2, 1))

    def one_block(args):
        q_blk, c_blk, p_blk = args
        s = jnp.einsum('bqhd,bkhd->bhqk', q_blk, k).astype(jnp.float32) * scale
        s = s + (c_blk[..., None] - ckT[:, :, None, :])
        mask = k_pos[None, :] <= p_blk[:, None]
        s = jnp.where(mask[None, None], s, -jnp.inf)
        pr = jax.nn.softmax(s, axis=-1)
        return jnp.einsum('bhqk,bkhd->bqhd', pr.astype(v.dtype), v)

    out = lax.map(one_block, (qs, cqs, ps))
    return out.transpose(1, 0, 2, 3, 4).reshape(B, Lq, H, Dh)


def mixer(h, pool_prefix, conv_prefix, past_k, past_v, past_logf, w_in, b_forget, q_norm, k_norm,
          w_pool, pool_scale, w_attn_out, conv_w, conv_b, conv_ln_g, conv_ln_b, w_conv_out, w_out):
    B, L, _ = h.shape
    offset = past_k.shape[1]
    z = h @ w_in
    q = rms_norm(z[..., C_Q:C_K].reshape(B, L, N_HEADS, HEAD_DIM), q_norm)
    k = rms_norm(z[..., C_K:C_V].reshape(B, L, N_HEADS, HEAD_DIM), k_norm)
    v = z[..., C_V:C_F].reshape(B, L, N_HEADS, HEAD_DIM)
    logf = jax.nn.log_sigmoid(z[..., C_F:C_GLU].astype(jnp.float32) + b_forget.astype(jnp.float32))
    k_all = jnp.concatenate([past_k.astype(k.dtype), k], axis=1)
    v_all = jnp.concatenate([past_v.astype(v.dtype), v], axis=1)
    c = jnp.cumsum(jnp.concatenate([past_logf.astype(jnp.float32), logf], axis=1), axis=1)
    q_pos = offset + jnp.arange(L)
    k_pos = jnp.arange(offset + L)
    o = fox_attention(q, k_all, v_all, c[:, offset:], c, q_pos, k_pos)
    y_b = o.reshape(B, L, ATTN_DIM) @ w_attn_out
    y_a, pool_state = pool_mixer(z[..., C_POOL:C_Q], pool_prefix, offset, w_pool, pool_scale)
    y_c, conv_state = conv_module(z[..., C_GLU:C_GATE], conv_prefix, conv_w, conv_b, conv_ln_g, conv_ln_b, w_conv_out)
    gates = jax.nn.sigmoid(z[..., C_GATE:].astype(jnp.float32)).astype(h.dtype).reshape(B, L, N_BRANCHES, D_MODEL)
    merged = gates[:, :, 0] * y_a + gates[:, :, 1] * y_b + gates[:, :, 2] * y_c
    return merged @ w_out, (k, v, logf, pool_state, conv_state)


def swiglu(h, w_gate, w_up, w_down):
    return (jax.nn.silu(h @ w_gate) * (h @ w_up)) @ w_down


def moe(h, router_w, w_gate, w_up, w_down):
    B, L, D = h.shape
    t = h.reshape(B * L, D)
    logits = (t @ router_w).astype(jnp.float32)
    top_v, top_i = lax.top_k(logits, TOP_K)
    w = jax.nn.softmax(top_v, axis=-1)
    comb = jnp.sum(jax.nn.one_hot(top_i, N_EXPERTS, dtype=jnp.float32) * w[..., None], axis=1)
    out = jnp.zeros_like(t)
    for e in range(N_EXPERTS):
        out = out + comb[:, e:e + 1].astype(t.dtype) * swiglu(t, w_gate[e], w_up[e], w_down[e])
    return out.reshape(B, L, D)


def block(x, p, pool_prefix, conv_prefix, past_k, past_v, past_logf, mix_w, ffn, g_mix, g_ffn, g_ple, w_ple, w_ple_gate):
    y, st = mixer(rms_norm(x, g_mix), pool_prefix, conv_prefix, past_k, past_v, past_logf, *mix_w)
    x = x + y
    x = x + ffn(rms_norm(x, g_ffn))
    gate = jax.nn.sigmoid((rms_norm(x, g_ple) @ w_ple_gate).astype(jnp.float32)).astype(x.dtype)
    x = x + gate * (p.astype(x.dtype) @ w_ple)
    return x, st


def setup_inputs(seed: int = 0) -> dict:
    key = jax.random.key(seed)
    ks = jax.random.split(key, 40)
    f32 = jnp.float32
    n_pages = PAST_LEN // PAGE_SIZE
    n_used = DEC_BATCH * n_pages
    n_pool = n_used + n_used // 4

    def nrm(k, shape, scale):
        return jax.random.normal(k, shape, f32) * scale

    def gain(k, shape):
        return 1.0 + 0.02 * jax.random.normal(k, shape, f32)

    page_table = jax.random.permutation(ks[0], n_pool)[:n_used].reshape(DEC_BATCH, n_pages).astype(jnp.int32)
    return {
        'x_prompt': nrm(ks[1], (BATCH, SEQ, D_MODEL), 1.0),
        'x_sample': nrm(ks[2], (DEC_BATCH, DEC_SEQ, D_MODEL), 1.0),
        'p_prompt': nrm(ks[3], (DEPTH, BATCH, SEQ, PLE_DIM), 1.0),
        'p_sample': nrm(ks[4], (DEPTH, DEC_BATCH, DEC_SEQ, PLE_DIM), 1.0),
        'cache_k': nrm(ks[5], (DEPTH, n_pool, PAGE_SIZE, N_HEADS, HEAD_DIM), 1.0),
        'cache_v': nrm(ks[6], (DEPTH, n_pool, PAGE_SIZE, N_HEADS, HEAD_DIM), 1.0),
        'cache_logf': jax.nn.log_sigmoid(jax.random.uniform(ks[7], (DEPTH, n_pool, PAGE_SIZE, N_HEADS), f32, 1.0, 6.0)),
        'state_pool': nrm(ks[8], (DEPTH, DEC_BATCH, POOL_STATE, POOL_DIM), 1.0),
        'state_conv': nrm(ks[9], (DEPTH, DEC_BATCH, CONV_STATE, CONV_DIM), 0.5),
        'page_table': page_table,
        'norm_mix': gain(ks[10], (DEPTH, D_MODEL)),
        'norm_ffn': gain(ks[11], (DEPTH, D_MODEL)),
        'norm_ple': gain(ks[12], (DEPTH, D_MODEL)),
        'w_in': nrm(ks[13], (DEPTH, D_MODEL, IN_COLS), D_MODEL ** -0.5),
        'b_forget': jax.random.uniform(ks[14], (DEPTH, N_HEADS), f32, 1.0, 6.0),
        'q_norm': gain(ks[15], (DEPTH, HEAD_DIM)),
        'k_norm': gain(ks[16], (DEPTH, HEAD_DIM)),
        'w_pool': nrm(ks[17], (DEPTH, N_POOL_GROUPS, POOL_GROUP, POOL_OUT_GROUP), POOL_GROUP ** -0.5),
        'pool_scale': gain(ks[18], (DEPTH, D_MODEL)),
        'w_attn_out': nrm(ks[19], (DEPTH, ATTN_DIM, D_MODEL), ATTN_DIM ** -0.5),
        'conv_w': nrm(ks[20], (DEPTH, CONV_WIDTH, CONV_DIM), CONV_WIDTH ** -0.5),
        'conv_b': nrm(ks[21], (DEPTH, CONV_DIM), 0.02),
        'conv_ln_g': gain(ks[22], (DEPTH, CONV_DIM)),
        'conv_ln_b': nrm(ks[23], (DEPTH, CONV_DIM), 0.02),
        'w_conv_out': nrm(ks[24], (DEPTH, CONV_DIM, D_MODEL), CONV_DIM ** -0.5),
        'w_out': nrm(ks[25], (DEPTH, D_MODEL, D_MODEL), D_MODEL ** -0.5),
        'dense_w_gate': nrm(ks[26], (N_DENSE, D_MODEL, FFN_DIM), D_MODEL ** -0.5),
        'dense_w_up': nrm(ks[27], (N_DENSE, D_MODEL, FFN_DIM), D_MODEL ** -0.5),
        'dense_w_down': nrm(ks[28], (N_DENSE, FFN_DIM, D_MODEL), FFN_DIM ** -0.5),
        'router_w': nrm(ks[29], (N_MOE, D_MODEL, N_EXPERTS), D_MODEL ** -0.5),
        'moe_w_gate': nrm(ks[30], (N_MOE, N_EXPERTS, D_MODEL, FFN_DIM), D_MODEL ** -0.5),
        'moe_w_up': nrm(ks[31], (N_MOE, N_EXPERTS, D_MODEL, FFN_DIM), D_MODEL ** -0.5),
        'moe_w_down': nrm(ks[32], (N_MOE, N_EXPERTS, FFN_DIM, D_MODEL), FFN_DIM ** -0.5),
        'w_ple': nrm(ks[33], (DEPTH, PLE_DIM, D_MODEL), PLE_DIM ** -0.5),
        'w_ple_gate': nrm(ks[34], (DEPTH, D_MODEL, D_MODEL), D_MODEL ** -0.5),
    }


def reference(x_prompt, x_sample, p_prompt, p_sample, cache_k, cache_v, cache_logf, state_pool, state_conv,
              page_table, norm_mix, norm_ffn, norm_ple, w_in, b_forget, q_norm, k_norm, w_pool, pool_scale,
              w_attn_out, conv_w, conv_b, conv_ln_g, conv_ln_b, w_conv_out, w_out, dense_w_gate, dense_w_up,
              dense_w_down, router_w, moe_w_gate, moe_w_up, moe_w_down, w_ple, w_ple_gate):
    xp, xs = x_prompt, x_sample
    bp = xp.shape[0]
    bs = xs.shape[0]
    n_pages = page_table.shape[1]
    past_len = n_pages * cache_k.shape[2]
    kp_l, vp_l, fp_l, pp_l, cp_l = [], [], [], [], []
    ks_l, vs_l, fs_l, ps_l, cs_l = [], [], [], [], []
    for i in range(DEPTH):
        mix_w = (w_in[i], b_forget[i], q_norm[i], k_norm[i], w_pool[i], pool_scale[i], w_attn_out[i],
                 conv_w[i], conv_b[i], conv_ln_g[i], conv_ln_b[i], w_conv_out[i], w_out[i])
        j = i // 2
        if i % 2 == 0:
            ffn = functools.partial(swiglu, w_gate=dense_w_gate[j], w_up=dense_w_up[j], w_down=dense_w_down[j])
        else:
            ffn = functools.partial(moe, router_w=router_w[j], w_gate=moe_w_gate[j], w_up=moe_w_up[j], w_down=moe_w_down[j])
        norms = (norm_mix[i], norm_ffn[i], norm_ple[i], w_ple[i], w_ple_gate[i])
        zk = jnp.zeros((bp, 0, N_HEADS, HEAD_DIM), xp.dtype)
        zf = jnp.zeros((bp, 0, N_HEADS), jnp.float32)
        xp, st_p = block(xp, p_prompt[i], jnp.zeros((bp, POOL_STATE, POOL_DIM), xp.dtype),
                         jnp.zeros((bp, CONV_STATE, CONV_DIM), xp.dtype), zk, zk, zf, mix_w, ffn, *norms)
        pk = cache_k[i][page_table].reshape(bs, past_len, N_HEADS, HEAD_DIM)
        pv = cache_v[i][page_table].reshape(bs, past_len, N_HEADS, HEAD_DIM)
        pf = cache_logf[i][page_table].reshape(bs, past_len, N_HEADS)
        xs, st_s = block(xs, p_sample[i], state_pool[i], state_conv[i], pk, pv, pf, mix_w, ffn, *norms)
        kp_l.append(st_p[0]); vp_l.append(st_p[1]); fp_l.append(st_p[2]); pp_l.append(st_p[3]); cp_l.append(st_p[4])
        ks_l.append(st_s[0]); vs_l.append(st_s[1]); fs_l.append(st_s[2]); ps_l.append(st_s[3]); cs_l.append(st_s[4])
    return (xp, xs,
            jnp.stack(kp_l), jnp.stack(vp_l), jnp.stack(fp_l), jnp.stack(pp_l), jnp.stack(cp_l),
            jnp.stack(ks_l), jnp.stack(vs_l), jnp.stack(fs_l), jnp.stack(ps_l), jnp.stack(cs_l))
```

```python
import functools

import jax
import jax.numpy as jnp
from jax import lax
from jax.experimental import pallas as pl
from jax.experimental.pallas import tpu as pltpu

F32 = jnp.float32
BF16 = jnp.bfloat16

HEAD_DIM = 64
POOL_WINDOWS = (2, 4, 8, 16)
POOL_GROUP = 128
POOL_STATE = max(POOL_WINDOWS) - 1
CONV_WIDTH = 31
CONV_STATE = CONV_WIDTH - 1
N_EXPERTS = 8
TOP_K = 2
RMS_EPS = 1e-6
LN_EPS = 1e-5
ATTN_SCALE = HEAD_DIM ** -0.5
LANES = 128

V7X_VMEM_LIMIT_BYTES = 56 * 1024 * 1024
ROW_TILE = 1024
MOE_SUB_ROWS = 256


def _params(sem):
    return pltpu.CompilerParams(dimension_semantics=sem, vmem_limit_bytes=V7X_VMEM_LIMIT_BYTES)


def _row_branches(i, n_full, bm, rem, do):
    if rem == 0:
        do(bm)
        return

    @pl.when(i < n_full)
    def _():
        do(bm)

    @pl.when(i == n_full)
    def _():
        do(rem)


def _rms(x, g):
    return x * lax.rsqrt(jnp.mean(x * x, axis=-1, keepdims=True) + RMS_EPS) * g


_NN = (((1,), (0,)), ((), ()))
_NT = (((1,), (1,)), ((), ()))


def _split(x):
    hi = x.astype(BF16)
    return hi, (x - hi.astype(F32)).astype(BF16)


def _dot3_parts(xh, xl, wh, wl, dims=_NN):
    m = xh.shape[0]
    both = lax.dot_general(jnp.concatenate([xh, xl], axis=0), wh, dims, preferred_element_type=F32)
    return both[:m] + (both[m:] + lax.dot_general(xh, wl, dims, preferred_element_type=F32))


def _dot3(x, w, dims=_NN):
    return _dot3_parts(*_split(x), *_split(w), dims)


def _norm_body(x_ref, g_ref, h_ref, *, bm, n_full, rem):
    def do(rows):
        h_ref[:rows] = _rms(x_ref[:rows], g_ref[...]).astype(h_ref.dtype)

    _row_branches(pl.program_id(0), n_full, bm, rem, do)


def rms_norm_rows(x, g3, layer, out_dtype=BF16, bm=512):
    T, D = x.shape
    bm = min(bm, T)
    n_full, rem = divmod(T, bm)
    return pl.pallas_call(
        functools.partial(_norm_body, bm=bm, n_full=n_full, rem=rem),
        out_shape=jax.ShapeDtypeStruct((T, D), out_dtype),
        grid=(pl.cdiv(T, bm),),
        in_specs=[pl.BlockSpec((bm, D), lambda i: (i, 0)),
                  pl.BlockSpec((None, 1, D), lambda i: (layer, 0, 0))],
        out_specs=pl.BlockSpec((bm, D), lambda i: (i, 0)),
        compiler_params=_params(("arbitrary",)),
        name="rms_norm_rows",
    )(x, g3)


def _add_norm_body(x_ref, y_ref, g_ref, x2_ref, h_ref, *, bm, n_full, rem):
    def do(rows):
        x2 = x_ref[:rows] + y_ref[:rows]
        x2_ref[:rows] = x2
        h_ref[:rows] = _rms(x2, g_ref[...]).astype(h_ref.dtype)

    _row_branches(pl.program_id(0), n_full, bm, rem, do)


def add_norm_rows(x, y, g3, layer, out_dtype=BF16, bm=512):
    T, D = x.shape
    bm = min(bm, T)
    n_full, rem = divmod(T, bm)
    spec = pl.BlockSpec((bm, D), lambda i: (i, 0))
    return pl.pallas_call(
        functools.partial(_add_norm_body, bm=bm, n_full=n_full, rem=rem),
        out_shape=(jax.ShapeDtypeStruct((T, D), F32), jax.ShapeDtypeStruct((T, D), out_dtype)),
        grid=(pl.cdiv(T, bm),),
        in_specs=[spec, spec, pl.BlockSpec((None, 1, D), lambda i: (layer, 0, 0))],
        out_specs=(spec, spec),
        compiler_params=_params(("arbitrary",)),
        name="add_norm_rows",
    )(x, y, g3)


def _mm_body(*refs, n_pairs, n_extra, n_out, cast_w, precise, epilogue, bm, n_full, rem):
    xs = refs[:n_pairs]
    ws = refs[n_pairs:2 * n_pairs]
    extras = refs[2 * n_pairs:2 * n_pairs + n_extra]
    outs = refs[2 * n_pairs + n_extra:2 * n_pairs + n_extra + n_out]
    wbs = refs[2 * n_pairs + n_extra + n_out:]
    j = pl.program_id(0)
    i = pl.program_id(1)

    @pl.when(i == 0)
    def _():
        k = 0
        for p in range(n_pairs):
            if cast_w[p]:
                wbs[k][...] = ws[p][...].astype(BF16)
                k += 1

    def do(rows):
        accs = []
        k = 0
        for p in range(n_pairs):
            x = xs[p][:rows]
            if precise:
                accs.append(_dot3(x, ws[p][...]))
                continue
            if x.dtype != BF16:
                x = x.astype(BF16)
            if cast_w[p]:
                w = wbs[k][...]
                k += 1
            else:
                w = ws[p][...]
            accs.append(jnp.dot(x, w, preferred_element_type=F32))
        epilogue(accs, rows, j, extras, outs)

    _row_branches(i, n_full, bm, rem, do)


def matmul_rows(pairs, extras, outs, epilogue, *, tn, n_tiles, name, bm, precise=False):
    T = pairs[0][0].shape[0]
    n_full, rem = divmod(T, bm)
    in_specs, args, cast_w, scratch = [], [], [], []
    for x, _, _, _ in pairs:
        in_specs.append(pl.BlockSpec((bm, x.shape[1]), lambda j, i: (i, 0)))
        args.append(x)
    for x, w, lead, off in pairs:
        K = x.shape[1]
        blk = (None,) * len(lead) + (K, tn)
        in_specs.append(pl.BlockSpec(blk, lambda j, i, lead=lead, off=off: (*lead, 0, j + off)))
        args.append(w)
        assert not precise or (w.dtype == F32 and x.dtype == F32)
        cast_w.append(w.dtype != BF16 and not precise)
        if cast_w[-1]:
            scratch.append(pltpu.VMEM((K, tn), BF16))
    for a, blk, imap in extras:
        in_specs.append(pl.BlockSpec(blk, imap))
        args.append(a)
    return pl.pallas_call(
        functools.partial(_mm_body, n_pairs=len(pairs), n_extra=len(extras), n_out=len(outs),
                          cast_w=tuple(cast_w), precise=precise, epilogue=epilogue, bm=bm, n_full=n_full,
                          rem=rem),
        out_shape=tuple(o[0] for o in outs),
        grid=(n_tiles, pl.cdiv(T, bm)),
        in_specs=in_specs,
        out_specs=tuple(pl.BlockSpec(o[1], o[2]) for o in outs),
        scratch_shapes=scratch,
        compiler_params=_params(("arbitrary", "arbitrary")),
        name=name,
    )(*args)


def _tile_spec(bm, tn, off=0):
    return (bm, tn), (lambda j, i, off=off: (i, j + off))


IN_TILE = 512


def _epi_pool_qkv(accs, rows, j, extras, outs):
    (acc,) = accs
    bd_ref, g_ref = extras
    (o_ref,) = outs
    is_head_norm = (j >= 1) & (j <= 4)

    @pl.when(is_head_norm)
    def _():
        hi, lo = _split(acc * acc)
        ss = (jnp.dot(hi, bd_ref[...], preferred_element_type=F32)
              + jnp.dot(lo, bd_ref[...], preferred_element_type=F32))
        o_ref[:rows] = acc * lax.rsqrt(ss * (1.0 / HEAD_DIM) + RMS_EPS) * g_ref[...]

    @pl.when(jnp.logical_not(is_head_norm))
    def _():
        o_ref[:rows] = acc


def project_pool_qkv(h, w_in, layer, q_norm, k_norm, precise=False):
    T = h.shape[0]
    tn = IN_TILE
    bm = min(ROW_TILE, T)
    n_tiles = 7
    heads_per_tile = tn // HEAD_DIM
    ones = jnp.ones((tn,), F32)
    gq = jnp.tile(q_norm[layer], heads_per_tile)
    gk = jnp.tile(k_norm[layer], heads_per_tile)
    gains = jnp.stack([ones, gq, gq, gk, gk, ones, ones]).reshape(n_tiles, 1, tn)
    grp = jnp.arange(tn) // HEAD_DIM
    bd = (grp[:, None] == grp[None, :]).astype(BF16)
    blk, imap = _tile_spec(bm, tn)
    (z1,) = matmul_rows(
        [(h, w_in, (layer,), 0)],
        [(bd, (tn, tn), lambda j, i: (0, 0)), (gains, (None, 1, tn), lambda j, i: (j, 0, 0))],
        [(jax.ShapeDtypeStruct((T, n_tiles * tn), F32), blk, imap)],
        _epi_pool_qkv, tn=tn, n_tiles=n_tiles, name="project_pool_qkv", bm=bm, precise=precise)
    return z1


def _epi_logf(accs, rows, j, extras, outs):
    (acc,) = accs
    (b_ref,) = extras
    (o_ref,) = outs
    o_ref[:rows] = jax.nn.log_sigmoid(acc + b_ref[...])


def project_logf(h, w_f, layer, b_forget3, precise=False):
    T = h.shape[0]
    H = w_f.shape[-1]
    bm = min(ROW_TILE, T)
    (lf,) = matmul_rows(
        [(h, w_f, (layer,), 0)],
        [(b_forget3, (None, 1, H), lambda j, i: (layer, 0, 0))],
        [(jax.ShapeDtypeStruct((T, H), F32), (bm, H), lambda j, i: (i, 0))],
        _epi_logf, tn=H, n_tiles=1, name="project_logf", bm=bm, precise=precise)
    return lf


def _epi_glu(accs, rows, j, extras, outs):
    a, b = accs
    (o_ref,) = outs
    o_ref[:rows] = a * jax.nn.sigmoid(b)


def project_glu(h, w_rest, layer, conv_dim, precise=False):
    T = h.shape[0]
    bm = min(ROW_TILE, T)
    (u,) = matmul_rows(
        [(h, w_rest, (layer,), 0), (h, w_rest, (layer,), 1)],
        [],
        [(jax.ShapeDtypeStruct((T, conv_dim), F32), (bm, conv_dim), lambda j, i: (i, 0))],
        _epi_glu, tn=conv_dim, n_tiles=1, name="project_glu", bm=bm, precise=precise)
    return u


def _epi_sigmoid(accs, rows, j, extras, outs):
    (acc,) = accs
    (o_ref,) = outs
    o_ref[:rows] = jax.nn.sigmoid(acc)


def project_gates(h, w_rest, layer, col0, n_cols, tn=1024, precise=False):
    T = h.shape[0]
    bm = min(ROW_TILE, T)
    blk, imap = _tile_spec(bm, tn)
    (g,) = matmul_rows(
        [(h, w_rest, (layer,), col0 // tn)],
        [],
        [(jax.ShapeDtypeStruct((T, n_cols), F32), blk, imap)],
        _epi_sigmoid, tn=tn, n_tiles=n_cols // tn, name="project_gates", bm=bm, precise=precise)
    return g


def _epi_merge(accs, rows, j, extras, outs):
    (yb,) = accs
    g0, g1, g2, ya, yc = extras
    (o_ref,) = outs
    merged = g0[:rows] * ya[:rows] + g1[:rows] * yb + g2[:rows] * yc[:rows]
    o_ref[:rows] = merged.astype(o_ref.dtype)


def attn_out_merge(o, w_attn_out, layer, gates, ya, yc, tn=1024, precise=False):
    T = o.shape[0]
    D = w_attn_out.shape[-1]
    nt = D // tn
    bm = min(ROW_TILE // 2, T)
    blk, imap = _tile_spec(bm, tn)
    (m,) = matmul_rows(
        [(o, w_attn_out, (layer,), 0)],
        [(gates, *_tile_spec(bm, tn, 0)), (gates, *_tile_spec(bm, tn, nt)),
         (gates, *_tile_spec(bm, tn, 2 * nt)), (ya, blk, imap), (yc, blk, imap)],
        [(jax.ShapeDtypeStruct((T, D), F32 if precise else BF16), blk, imap)],
        _epi_merge, tn=tn, n_tiles=nt, name="attn_out_merge", bm=bm, precise=precise)
    return m


def _epi_residual(accs, rows, j, extras, outs):
    (acc,) = accs
    (x_ref,) = extras
    (o_ref,) = outs
    o_ref[:rows] = x_ref[:rows] + acc


def out_proj_residual(merged, w_out, layer, x, tn=1024, precise=False):
    T, D = x.shape
    bm = min(ROW_TILE, T)
    blk, imap = _tile_spec(bm, tn)
    (x2,) = matmul_rows(
        [(merged, w_out, (layer,), 0)],
        [(x, blk, imap)],
        [(jax.ShapeDtypeStruct((T, D), F32), blk, imap)],
        _epi_residual, tn=tn, n_tiles=D // tn, name="out_proj_residual", bm=bm, precise=precise)
    return x2


def _epi_ple(accs, rows, j, extras, outs):
    gate_logit, emb = accs
    (x_ref,) = extras
    (o_ref,) = outs
    o_ref[:rows] = x_ref[:rows] + jax.nn.sigmoid(gate_logit) * emb


def ple_residual(h, w_ple_gate, p, w_ple, layer, x, tn=1024, precise=False):
    T, D = x.shape
    bm = min(ROW_TILE // 2, T)
    blk, imap = _tile_spec(bm, tn)
    (x2,) = matmul_rows(
        [(h, w_ple_gate, (layer,), 0), (p, w_ple, (layer,), 0)],
        [(x, blk, imap)],
        [(jax.ShapeDtypeStruct((T, D), F32), blk, imap)],
        _epi_ple, tn=tn, n_tiles=D // tn, name="ple_residual", bm=bm, precise=precise)
    return x2


POOL_PAD = 16
CONV_PAD = 32


def _pool_body(u_ref, pre_ref, w_ref, sc_ref, ya_ref, st_ref, ext_ref, *, nb, L, R, offset, precise):
    C = ext_ref.shape[-1]
    c = pl.program_id(1)

    @pl.when(c == 0)
    def _():
        for lb in range(nb):
            ext_ref[lb, 0:POOL_PAD - POOL_STATE, :] = jnp.zeros((POOL_PAD - POOL_STATE, C), F32)
            ext_ref[lb, POOL_PAD - POOL_STATE:POOL_PAD, :] = pre_ref[lb]
            ext_ref[lb, POOL_PAD:POOL_PAD + L, :] = u_ref[lb * L:(lb + 1) * L, :]
            st_ref[lb] = ext_ref[lb, POOL_PAD + L - POOL_STATE:POOL_PAD + L, :]

    t0 = 0 if L == R else pl.multiple_of(c * R, R)
    pos = offset + t0 + lax.broadcasted_iota(jnp.int32, (R, 1), 0)
    wins = [ext_ref[lb, pl.ds(t0, R + POOL_PAD), :] for lb in range(nb)]
    for g, w in enumerate(POOL_WINDOWS):
        cnt = jnp.minimum(pos + 1, w).astype(F32)
        rs = []
        for lb in range(nb):
            wg = wins[lb][:, g * POOL_GROUP:(g + 1) * POOL_GROUP]
            cur = wg[POOL_PAD:POOL_PAD + R]
            acc = cur
            for i in range(1, w):
                acc = acc + wg[POOL_PAD - i:POOL_PAD - i + R]
            rs.append(acc / cnt - cur)
        r = rs[0] if nb == 1 else jnp.concatenate(rs, axis=0)
        if precise:
            y = _dot3(r, w_ref[g])
        else:
            y = jnp.dot(r.astype(BF16), w_ref[g].astype(BF16), preferred_element_type=F32)
        og = y.shape[1]
        ya_ref[:, g * og:(g + 1) * og] = y * sc_ref[:, g * og:(g + 1) * og]


def pool_mixer_rows(z1, nseq, L, nb, prefix, w_pool, pool_scale3, layer, offset, d_model, precise=False):
    C = len(POOL_WINDOWS) * POOL_GROUP
    R = min(L, 512)
    n_chunks = L // R
    return pl.pallas_call(
        functools.partial(_pool_body, nb=nb, L=L, R=R, offset=offset, precise=precise),
        out_shape=(jax.ShapeDtypeStruct((nseq * L, d_model), F32),
                   jax.ShapeDtypeStruct((nseq, POOL_STATE, C), F32)),
        grid=(nseq // nb, n_chunks),
        in_specs=[pl.BlockSpec((nb * L, C), lambda s, c: (s, 0)),
                  pl.BlockSpec((nb, POOL_STATE, C), lambda s, c: (s, 0, 0)),
                  pl.BlockSpec((None,) + w_pool.shape[1:], lambda s, c: (layer, 0, 0, 0)),
                  pl.BlockSpec((None, 1, d_model), lambda s, c: (layer, 0, 0))],
        out_specs=(pl.BlockSpec((nb * R, d_model), lambda s, c: (s * n_chunks + c, 0)),
                   pl.BlockSpec((nb, POOL_STATE, C), lambda s, c: (s, 0, 0))),
        scratch_shapes=[pltpu.VMEM((nb, POOL_PAD + L, C), F32)],
        compiler_params=_params(("arbitrary", "arbitrary")),
        name="pool_mixer_rows",
    )(z1, prefix, w_pool, pool_scale3)


def _conv_body(u_ref, pre_ref, cw_ref, cb_ref, lg_ref, lb_ref, wo_ref, yc_ref, st_ref, ext_ref, *, nb, L, R,
               precise):
    C = ext_ref.shape[-1]
    c = pl.program_id(1)

    @pl.when(c == 0)
    def _():
        for lb in range(nb):
            ext_ref[lb, 0:CONV_PAD - CONV_STATE, :] = jnp.zeros((CONV_PAD - CONV_STATE, C), F32)
            ext_ref[lb, CONV_PAD - CONV_STATE:CONV_PAD, :] = pre_ref[lb]
            ext_ref[lb, CONV_PAD:CONV_PAD + L, :] = u_ref[lb * L:(lb + 1) * L, :]
            st_ref[lb] = ext_ref[lb, CONV_PAD + L - CONV_STATE:CONV_PAD + L, :]

    t0 = 0 if L == R else pl.multiple_of(c * R, R)
    ys = []
    for lb in range(nb):
        win = ext_ref[lb, pl.ds(t0, R + CONV_PAD), :]
        acc = jnp.zeros((R, C), F32)
        for j in range(CONV_WIDTH):
            s0 = CONV_PAD - CONV_STATE + j
            acc = acc + win[s0:s0 + R] * cw_ref[j:j + 1, :]
        ys.append(acc)
    y = (ys[0] if nb == 1 else jnp.concatenate(ys, axis=0)) + cb_ref[...]
    mu = jnp.mean(y, axis=-1, keepdims=True)
    yc = y - mu
    yn = yc * lax.rsqrt(jnp.mean(yc * yc, axis=-1, keepdims=True) + LN_EPS) * lg_ref[...] + lb_ref[...]
    act = yn * jax.nn.sigmoid(yn)
    if precise:
        yc_ref[...] = _dot3(act, wo_ref[...])
    else:
        yc_ref[...] = jnp.dot(act.astype(BF16), wo_ref[...].astype(BF16), preferred_element_type=F32)


def conv_module_rows(u, nseq, L, nb, prefix, conv_w, conv_b3, ln_g3, ln_b3, w_conv_out, layer, precise=False):
    C = u.shape[1]
    D = w_conv_out.shape[-1]
    R = min(L, 256)
    n_chunks = L // R
    vec = lambda a: pl.BlockSpec((None, 1, a.shape[-1]), lambda s, c: (layer, 0, 0))
    return pl.pallas_call(
        functools.partial(_conv_body, nb=nb, L=L, R=R, precise=precise),
        out_shape=(jax.ShapeDtypeStruct((nseq * L, D), F32), jax.ShapeDtypeStruct((nseq, CONV_STATE, C), F32)),
        grid=(nseq // nb, n_chunks),
        in_specs=[pl.BlockSpec((nb * L, C), lambda s, c: (s, 0)),
                  pl.BlockSpec((nb, CONV_STATE, C), lambda s, c: (s, 0, 0)),
                  pl.BlockSpec((None, CONV_WIDTH, C), lambda s, c: (layer, 0, 0)),
                  vec(conv_b3), vec(ln_g3), vec(ln_b3),
                  pl.BlockSpec((None, C, D), lambda s, c: (layer, 0, 0))],
        out_specs=(pl.BlockSpec((nb * R, D), lambda s, c: (s * n_chunks + c, 0)),
                   pl.BlockSpec((nb, CONV_STATE, C), lambda s, c: (s, 0, 0))),
        scratch_shapes=[pltpu.VMEM((nb, CONV_PAD + L, C), F32)],
        compiler_params=_params(("arbitrary", "arbitrary")),
        name="conv_module_rows",
    )(u, prefix, conv_w, conv_b3, ln_g3, ln_b3, w_conv_out)


CUMSUM_CHUNK = 128


def _cumsum_body(x_ref, o_ref, *, L):
    n = CUMSUM_CHUNK
    tri = (lax.broadcasted_iota(jnp.int32, (n, n), 0) >= lax.broadcasted_iota(jnp.int32, (n, n), 1)).astype(F32)
    carry = jnp.zeros((1, x_ref.shape[-1]), F32)
    for k in range(L // n):
        ck = jnp.dot(tri, x_ref[k * n:(k + 1) * n, :], precision=lax.Precision.HIGHEST,
                     preferred_element_type=F32) + carry
        o_ref[k * n:(k + 1) * n, :] = ck
        carry = ck[n - 1:n, :]


def cumsum_rows(logf, nseq, L):
    H = logf.shape[1]
    return pl.pallas_call(
        functools.partial(_cumsum_body, L=L),
        out_shape=jax.ShapeDtypeStruct((nseq * L, H), F32),
        grid=(nseq,),
        in_specs=[pl.BlockSpec((L, H), lambda b: (b, 0))],
        out_specs=pl.BlockSpec((L, H), lambda b: (b, 0)),
        compiler_params=_params(("arbitrary",)),
        name="cumsum_rows",
    )(logf)


def _flash_body(q_ref, k_ref, v_ref, cq_ref, ck_ref, o_ref, m_ref, l_ref, acc_ref, *, blk):
    qi = pl.program_id(2)
    q2 = q_ref[...] * ATTN_SCALE
    first = lax.broadcasted_iota(jnp.int32, (1, 2 * HEAD_DIM), 1) < HEAD_DIM
    qh = (jnp.where(first, q2, 0.0).astype(BF16), jnp.where(first, 0.0, q2).astype(BF16))
    cq = cq_ref[...]
    m_ref[...] = jnp.full(m_ref.shape, -jnp.inf, F32)
    l_ref[...] = jnp.zeros(l_ref.shape, F32)
    acc_ref[...] = jnp.zeros(acc_ref.shape, F32)
    causal = (lax.broadcasted_iota(jnp.int32, (blk, blk), 0) >= lax.broadcasted_iota(jnp.int32, (blk, blk), 1))

    def chunk(kc, masked):
        k0 = pl.multiple_of(kc * blk, blk)
        k2 = k_ref[pl.ds(k0, blk), :].astype(BF16)
        v2 = v_ref[pl.ds(k0, blk), :].astype(BF16)
        ck = ck_ref[:, pl.ds(k0, blk)]
        for h in range(2):
            s = lax.dot_general(qh[h], k2, _NT, preferred_element_type=F32)
            s = s + (cq[:, h:h + 1] - ck[h:h + 1, :])
            if masked:
                s = jnp.where(causal, s, -jnp.inf)
            m_prev = m_ref[h]
            m_new = jnp.maximum(m_prev, jnp.max(s, axis=1, keepdims=True))
            alpha = jnp.exp(m_prev - m_new)
            p = jnp.exp(s - m_new)
            l_ref[h] = alpha * l_ref[h] + jnp.sum(p, axis=1, keepdims=True)
            acc_ref[h] = alpha * acc_ref[h] + jnp.dot(p.astype(BF16), v2, preferred_element_type=F32)
            m_ref[h] = m_new

    def body(kc, carry):
        chunk(kc, False)
        return carry

    lax.fori_loop(0, qi, body, 0)
    chunk(qi, True)
    out = jnp.where(first, acc_ref[0] / l_ref[0], acc_ref[1] / l_ref[1])
    o_ref[...] = out.astype(o_ref.dtype)


def prompt_attention(z1, c_pairs, ct_pairs, nseq, L, n_heads, q_col, k_col, v_col):
    blk = min(L, 512)
    nq = L // blk
    lanes = 2 * HEAD_DIM
    qc, kc, vc = q_col // lanes, k_col // lanes, v_col // lanes
    return pl.pallas_call(
        functools.partial(_flash_body, blk=blk),
        out_shape=jax.ShapeDtypeStruct((nseq * L, n_heads * HEAD_DIM), BF16),
        grid=(nseq, n_heads // 2, nq),
        in_specs=[pl.BlockSpec((blk, lanes), lambda b, hp, qi: (b * nq + qi, qc + hp)),
                  pl.BlockSpec((L, lanes), lambda b, hp, qi: (b, kc + hp)),
                  pl.BlockSpec((L, lanes), lambda b, hp, qi: (b, vc + hp)),
                  pl.BlockSpec((None, None, blk, 2), lambda b, hp, qi: (b, hp, qi, 0)),
                  pl.BlockSpec((None, None, 2, L), lambda b, hp, qi: (b, hp, 0, 0))],
        out_specs=pl.BlockSpec((blk, lanes), lambda b, hp, qi: (b * nq + qi, hp)),
        scratch_shapes=[pltpu.VMEM((2, blk, 1), F32), pltpu.VMEM((2, blk, 1), F32),
                        pltpu.VMEM((2, blk, lanes), F32)],
        compiler_params=_params(("arbitrary", "arbitrary", "arbitrary")),
        name="prompt_attention",
    )(z1, z1, z1, c_pairs, ct_pairs)


DECODE_PAGES_PER_STEP = 4


def _decode_body(pt_ref, q_ref, kn_ref, vn_ref, lfn_ref, *rest, PG, H, page):
    k_refs, v_refs, lf_refs = rest[:PG], rest[PG:2 * PG], rest[2 * PG:3 * PG]
    o_ref, m_ref, l_ref, acc_ref, carry_ref, cq_ref = rest[3 * PG:]
    s = pl.program_id(1)
    TH = q_ref.shape[0]
    qh, ql = _split(q_ref[...] * ATTN_SCALE)
    lane = lax.broadcasted_iota(jnp.int32, (1, LANES), 1)
    sh_h = H.bit_length() - 1

    @pl.when(s == 0)
    def _():
        cn = lfn_ref[...]
        sh = H
        while sh < TH:
            cn = cn + jnp.where(lane >= sh, pltpu.roll(cn, sh, axis=1), 0.0)
            sh *= 2
        cn = cn[:, :TH]
        rr = lax.broadcasted_iota(jnp.int32, (TH, TH), 0)
        ll = lax.broadcasted_iota(jnp.int32, (TH, TH), 1)
        cq = jnp.sum(jnp.where(rr == ll, cn, 0.0), axis=1, keepdims=True)
        cq_ref[...] = cq
        sc = _dot3_parts(qh, ql, *_split(kn_ref[...]), _NT) + (cq - cn)
        valid = ((ll & (H - 1)) == (rr & (H - 1))) & ((ll >> sh_h) <= (rr >> sh_h))
        sc = jnp.where(valid, sc, -jnp.inf)
        m = jnp.max(sc, axis=1, keepdims=True)
        p = jnp.exp(sc - m)
        m_ref[...] = m
        l_ref[...] = jnp.sum(p, axis=1, keepdims=True)
        acc_ref[...] = _dot3(p, vn_ref[...])
        carry_ref[...] = jnp.zeros(carry_ref.shape, F32)

    PL = page * H
    n_rows = PL // LANES
    same_head = ((lax.broadcasted_iota(jnp.int32, (TH, PL), 1) & (H - 1))
                 == (lax.broadcasted_iota(jnp.int32, (TH, PL), 0) & (H - 1)))
    later_row = (lax.broadcasted_iota(jnp.int32, (n_rows, n_rows), 1)
                 > lax.broadcasted_iota(jnp.int32, (n_rows, n_rows), 0)).astype(F32)
    cq = cq_ref[...]
    carry = carry_ref[...]
    scores = []
    for r in range(PG):
        lf = lf_refs[r][...]
        nxt = jnp.where(lane < LANES - H, pltpu.roll(lf, LANES - H, axis=1), 0.0)
        sh = H
        while sh < LANES:
            nxt = nxt + jnp.where(lane < LANES - sh, pltpu.roll(nxt, LANES - sh, axis=1), 0.0)
            sh *= 2
        tot = lf
        sh = H
        while sh < LANES:
            tot = tot + pltpu.roll(tot, sh, axis=1)
            sh *= 2
        after = jnp.dot(later_row, tot, precision=lax.Precision.HIGHEST, preferred_element_type=F32)
        bias2 = nxt + after + carry
        carry = carry + jnp.sum(tot, axis=0, keepdims=True)
        bias = jnp.concatenate([bias2[i:i + 1, :] for i in range(n_rows)], axis=1)
        k2 = k_refs[r][...].reshape(PL, HEAD_DIM)
        sc = _dot3_parts(qh, ql, *_split(k2), _NT) + bias + cq
        scores.append(jnp.where(same_head, sc, -jnp.inf))
    carry_ref[...] = carry

    m_prev = m_ref[...]
    m_new = m_prev
    for sc in scores:
        m_new = jnp.maximum(m_new, jnp.max(sc, axis=1, keepdims=True))
    alpha = jnp.exp(m_prev - m_new)
    l_new = alpha * l_ref[...]
    acc = alpha * acc_ref[...]
    for r in range(PG):
        p = jnp.exp(scores[r] - m_new)
        l_new = l_new + jnp.sum(p, axis=1, keepdims=True)
        acc = acc + _dot3(p, v_refs[r][...].reshape(PL, HEAD_DIM))
    m_ref[...] = m_new
    l_ref[...] = l_new
    acc_ref[...] = acc

    @pl.when(s == pl.num_programs(1) - 1)
    def _():
        o_ref[...] = acc / l_new


def sample_attention(q3, kn3, vn3, lfn3, cache_k, cache_v, cache_lf2, page_table, layer):
    B, TH, Dh = q3.shape
    _, _, page, H, _ = cache_k.shape
    n_pages = page_table.shape[1]
    PG = DECODE_PAGES_PER_STEP
    assert n_pages % PG == 0 and (page * H) % LANES == 0 and H & (H - 1) == 0 and H <= LANES and TH <= LANES
    n_rows = page * H // LANES

    def page_map(r):
        return lambda b, s, pt: (layer, pt[b, n_pages - 1 - (s * PG + r)], 0, 0, 0)

    def lf_map(r):
        return lambda b, s, pt: (layer, pt[b, n_pages - 1 - (s * PG + r)], 0, 0)

    small = pl.BlockSpec((None, TH, Dh), lambda b, s, pt: (b, 0, 0))
    in_specs = [small, small, small, pl.BlockSpec((None, 1, LANES), lambda b, s, pt: (b, 0, 0))]
    in_specs += [pl.BlockSpec((None, None, page, H, Dh), page_map(r)) for r in range(PG)]
    in_specs += [pl.BlockSpec((None, None, page, H, Dh), page_map(r)) for r in range(PG)]
    in_specs += [pl.BlockSpec((None, None, n_rows, LANES), lf_map(r)) for r in range(PG)]
    return pl.pallas_call(
        functools.partial(_decode_body, PG=PG, H=H, page=page),
        out_shape=jax.ShapeDtypeStruct((B, TH, Dh), F32),
        grid_spec=pltpu.PrefetchScalarGridSpec(
            num_scalar_prefetch=1,
            grid=(B, n_pages // PG),
            in_specs=in_specs,
            out_specs=pl.BlockSpec((None, TH, Dh), lambda b, s, pt: (b, 0, 0)),
            scratch_shapes=[pltpu.VMEM((TH, 1), F32), pltpu.VMEM((TH, 1), F32), pltpu.VMEM((TH, Dh), F32),
                            pltpu.VMEM((1, LANES), F32), pltpu.VMEM((TH, 1), F32)]),
        compiler_params=_params(("arbitrary", "arbitrary")),
        name="sample_attention",
    )(page_table, q3, kn3, vn3, lfn3, *([cache_k] * PG), *([cache_v] * PG), *([cache_lf2] * PG))


FFN_CHUNK = 512


def _ffn_body(te_ref, tr_ref, x_ref, wg_ref, wu_ref, wd_ref, o_ref, *scratch, bm, sub, precise):
    s = pl.program_id(0)
    f = pl.program_id(1)
    rows_valid = tr_ref[s]

    def accumulate(r0, n, y):
        @pl.when(f == 0)
        def _():
            o_ref[r0:r0 + n] = y

        @pl.when(f > 0)
        def _():
            o_ref[r0:r0 + n] += y

    if precise:
        @pl.when(rows_valid > 0)
        def _():
            x = x_ref[...]
            g = _dot3(x, wg_ref[...])
            u = _dot3(x, wu_ref[...])
            accumulate(0, bm, _dot3(g * jax.nn.sigmoid(g) * u, wd_ref[...]))
    else:
        wgb, wub, wdb = scratch

        def block(r0, n):
            x = x_ref[r0:r0 + n]
            g = jnp.dot(x, wgb[...], preferred_element_type=F32)
            u = jnp.dot(x, wub[...], preferred_element_type=F32)
            a = (g * jax.nn.sigmoid(g) * u).astype(BF16)
            accumulate(r0, n, jnp.dot(a, wdb[...], preferred_element_type=F32))

        @pl.when(rows_valid > 0)
        def _():
            wgb[...] = wg_ref[...].astype(BF16)
            wub[...] = wu_ref[...].astype(BF16)
            wdb[...] = wd_ref[...].astype(BF16)
            for sb in range(bm // sub):
                @pl.when(sb * sub < rows_valid)
                def _():
                    block(sb * sub, sub)

                @pl.when((sb * sub >= rows_valid) & (f == 0))
                def _():
                    o_ref[sb * sub:(sb + 1) * sub] = jnp.zeros((sub, o_ref.shape[1]), F32)

    @pl.when((rows_valid == 0) & (f == 0))
    def _():
        o_ref[...] = jnp.zeros(o_ref.shape, F32)


def ffn_tiles(x, tile_expert, tile_rows, w_gate, w_up, w_down, lw, bm=ROW_TILE, precise=False):
    D = x.shape[1]
    F = w_gate.shape[-1]
    tf = FFN_CHUNK
    nf = F // tf
    bm = min(bm, x.shape[0])
    n_tiles = tile_expert.shape[0]
    assert precise or x.shape[0] == n_tiles * bm

    def f_eff(s, f, tr):
        return jnp.where(tr[s] > 0, f, nf - 1)

    x_spec = (pl.BlockSpec((bm, D), lambda s, f, te, tr: (0, 0)) if precise else
              pl.BlockSpec((bm, D), lambda s, f, te, tr: (s, 0), pipeline_mode=pl.Buffered(1)))
    in_specs = [x_spec,
                pl.BlockSpec((None, None, D, tf), lambda s, f, te, tr: (lw, te[s], 0, f_eff(s, f, tr))),
                pl.BlockSpec((None, None, D, tf), lambda s, f, te, tr: (lw, te[s], 0, f_eff(s, f, tr))),
                pl.BlockSpec((None, None, tf, D), lambda s, f, te, tr: (lw, te[s], f_eff(s, f, tr), 0))]
    scratch = [] if precise else [pltpu.VMEM((D, tf), BF16), pltpu.VMEM((D, tf), BF16),
                                  pltpu.VMEM((tf, D), BF16)]
    return pl.pallas_call(
        functools.partial(_ffn_body, bm=bm, sub=min(MOE_SUB_ROWS, bm), precise=precise),
        out_shape=jax.ShapeDtypeStruct((n_tiles * bm, D), F32),
        grid_spec=pltpu.PrefetchScalarGridSpec(
            num_scalar_prefetch=2,
            grid=(n_tiles, nf),
            in_specs=in_specs,
            out_specs=pl.BlockSpec((bm, D), lambda s, f, te, tr: (s, 0)),
            scratch_shapes=scratch),
        compiler_params=_params(("arbitrary", "arbitrary")),
        name="ffn_tiles",
    )(tile_expert, tile_rows, x, w_gate, w_up, w_down)


def _router_body(x_ref, g_ref, wr_ref, h_ref, idx_ref, wt_ref, *, bm, n_full, rem):
    def do(rows):
        h = _rms(x_ref[:rows], g_ref[...])
        h_ref[:rows] = h
        logits = jnp.dot(h, wr_ref[...], precision=lax.Precision.HIGHEST, preferred_element_type=F32)
        E = logits.shape[1]
        lane = lax.broadcasted_iota(jnp.int32, (rows, E), 1)
        m1 = jnp.max(logits, axis=1, keepdims=True)
        i1 = jnp.min(jnp.where(logits == m1, lane, E), axis=1, keepdims=True)
        rest = jnp.where(lane == i1, -jnp.inf, logits)
        m2 = jnp.max(rest, axis=1, keepdims=True)
        i2 = jnp.min(jnp.where(rest == m2, lane, E), axis=1, keepdims=True)
        e = jnp.exp(m2 - m1)
        w1 = 1.0 / (1.0 + e)
        slot = lax.broadcasted_iota(jnp.int32, (rows, TOP_K), 1)
        idx_ref[:rows] = jnp.where(slot == 0, i1, i2)
        wt_ref[:rows] = jnp.where(slot == 0, w1, e * w1)

    _row_branches(pl.program_id(0), n_full, bm, rem, do)


def route_rows(x, g3, layer, router_w, lw, bm=512):
    T, D = x.shape
    E = router_w.shape[-1]
    bm = min(bm, T)
    n_full, rem = divmod(T, bm)
    return pl.pallas_call(
        functools.partial(_router_body, bm=bm, n_full=n_full, rem=rem),
        out_shape=(jax.ShapeDtypeStruct((T, D), F32), jax.ShapeDtypeStruct((T, TOP_K), jnp.int32),
                   jax.ShapeDtypeStruct((T, TOP_K), F32)),
        grid=(pl.cdiv(T, bm),),
        in_specs=[pl.BlockSpec((bm, D), lambda i: (i, 0)),
                  pl.BlockSpec((None, 1, D), lambda i: (layer, 0, 0)),
                  pl.BlockSpec((None, D, E), lambda i: (lw, 0, 0))],
        out_specs=(pl.BlockSpec((bm, D), lambda i: (i, 0)), pl.BlockSpec((bm, TOP_K), lambda i: (i, 0)),
                   pl.BlockSpec((bm, TOP_K), lambda i: (i, 0))),
        compiler_params=_params(("arbitrary",)),
        name="route_rows",
    )(x, g3, router_w)


def moe_dispatch(idx, bm, n_tiles):
    T = idx.shape[0]
    e_flat = idx.reshape(-1)
    onehot = (e_flat[:, None] == jnp.arange(N_EXPERTS, dtype=jnp.int32)[None, :]).astype(jnp.int32)
    incl = jnp.cumsum(onehot, axis=0)
    counts = incl[-1]
    rank = jnp.sum((incl - onehot) * onehot, axis=1)
    tiles_e = (counts + bm - 1) // bm
    tile_end = jnp.cumsum(tiles_e)
    tile_start = tile_end - tiles_e
    pos = (tile_start[e_flat] * bm + rank).astype(jnp.int32)
    src = jnp.zeros((n_tiles * bm,), jnp.int32).at[pos].set(jnp.arange(TOP_K * T, dtype=jnp.int32) // TOP_K)
    tile_ids = jnp.arange(n_tiles, dtype=jnp.int32)
    used = tile_ids < tile_end[-1]
    owner = jnp.sum((tile_ids[:, None] >= tile_end[None, :]).astype(jnp.int32), axis=1)
    owner = jnp.where(used, owner, owner[jnp.maximum(tile_end[-1] - 1, 0)])
    rows = jnp.clip(counts[owner] - (tile_ids - tile_start[owner]) * bm, 0, bm)
    rows = jnp.where(used, rows, 0).astype(jnp.int32)
    return pos, src, owner.astype(jnp.int32), rows


GATHER_ROWS = 256


def _row_copy(src_hbm, src_row, dst_vmem, dst_row, sem):
    return pltpu.make_async_copy(src_hbm.at[pl.ds(src_row, 1)], dst_vmem.at[pl.ds(dst_row, 1)], sem)


def _gather_body(src_ref, h_hbm, o_ref, buf, sem):
    base = pl.program_id(0) * GATHER_ROWS

    def issue(r, carry):
        _row_copy(h_hbm, src_ref[base + r], buf, r, sem).start()
        return carry

    def drain(r, carry):
        _row_copy(h_hbm, 0, buf, r, sem).wait()
        return carry

    lax.fori_loop(0, GATHER_ROWS, issue, 0)
    lax.fori_loop(0, GATHER_ROWS, drain, 0)
    o_ref[...] = buf[...].astype(o_ref.dtype)


def gather_rows(h, src):
    n = src.shape[0]
    D = h.shape[1]
    return pl.pallas_call(
        _gather_body,
        out_shape=jax.ShapeDtypeStruct((n, D), BF16),
        grid_spec=pltpu.PrefetchScalarGridSpec(
            num_scalar_prefetch=1,
            grid=(n // GATHER_ROWS,),
            in_specs=[pl.BlockSpec(memory_space=pl.ANY)],
            out_specs=pl.BlockSpec((GATHER_ROWS, D), lambda i, src: (i, 0)),
            scratch_shapes=[pltpu.VMEM((GATHER_ROWS, D), F32), pltpu.SemaphoreType.DMA]),
        compiler_params=_params(("arbitrary",)),
        name="gather_rows",
    )(src, h)


def _combine_body(pos_ref, y_hbm, wt_ref, x_ref, g_ref, x2_ref, h_ref, buf, sem, *, bm, n_full, rem):
    i = pl.program_id(0)

    def do(rows):
        base = i * bm

        def issue(r, carry):
            for k in range(TOP_K):
                _row_copy(y_hbm, pos_ref[TOP_K * (base + r) + k], buf.at[k], r, sem).start()
            return carry

        def drain(r, carry):
            for k in range(TOP_K):
                _row_copy(y_hbm, 0, buf.at[k], r, sem).wait()
            return carry

        lax.fori_loop(0, rows, issue, 0)
        lax.fori_loop(0, rows, drain, 0)
        w = wt_ref[:rows]
        y = w[:, 0:1] * buf[0, :rows] + w[:, 1:2] * buf[1, :rows]
        x2 = x_ref[:rows] + y
        x2_ref[:rows] = x2
        h_ref[:rows] = _rms(x2, g_ref[...]).astype(h_ref.dtype)

    _row_branches(i, n_full, bm, rem, do)


def combine_rows(y_sorted, pos, wts, x, g3, layer, bm=256):
    T, D = x.shape
    n_full, rem = divmod(T, bm)
    row = lambda i, pos: (i, 0)
    return pl.pallas_call(
        functools.partial(_combine_body, bm=bm, n_full=n_full, rem=rem),
        out_shape=(jax.ShapeDtypeStruct((T, D), F32), jax.ShapeDtypeStruct((T, D), BF16)),
        grid_spec=pltpu.PrefetchScalarGridSpec(
            num_scalar_prefetch=1,
            grid=(pl.cdiv(T, bm),),
            in_specs=[pl.BlockSpec(memory_space=pl.ANY),
                      pl.BlockSpec((bm, TOP_K), row), pl.BlockSpec((bm, D), row),
                      pl.BlockSpec((None, 1, D), lambda i, pos: (layer, 0, 0))],
            out_specs=(pl.BlockSpec((bm, D), row), pl.BlockSpec((bm, D), row)),
            scratch_shapes=[pltpu.VMEM((TOP_K, bm, D), F32), pltpu.SemaphoreType.DMA]),
        compiler_params=_params(("arbitrary",)),
        name="combine_rows",
    )(pos, y_sorted, wts, x, g3)


def _combine_experts_body(y_ref, idx_ref, wt_ref, x_ref, g_ref, x2_ref, h_ref, *, n_experts):
    rows = x_ref.shape[0]
    idx = idx_ref[...]
    wt = wt_ref[...]
    acc = jnp.zeros(x_ref.shape, F32)
    for e in range(n_experts):
        ce = jnp.sum(jnp.where(idx == e, wt, 0.0), axis=1, keepdims=True)
        acc = acc + ce * y_ref[e * rows:(e + 1) * rows, :]
    x2 = x_ref[...] + acc
    x2_ref[...] = x2
    h_ref[...] = _rms(x2, g_ref[...])


def combine_experts(y, idx, wts, x, g3, layer):
    T, D = x.shape
    E = y.shape[0] // T
    full = lambda a: pl.BlockSpec(a.shape, lambda i: (0,) * a.ndim)
    return pl.pallas_call(
        functools.partial(_combine_experts_body, n_experts=E),
        out_shape=(jax.ShapeDtypeStruct((T, D), F32), jax.ShapeDtypeStruct((T, D), F32)),
        grid=(1,),
        in_specs=[full(y), full(idx), full(wts), full(x), pl.BlockSpec((None, 1, D), lambda i: (layer, 0, 0))],
        out_specs=(full(x), full(x)),
        compiler_params=_params(("arbitrary",)),
        name="combine_experts",
    )(y, idx, wts, x, g3)


def _forward(x_prompt, x_sample, p_prompt, p_sample, cache_k, cache_v, cache_logf, state_pool, state_conv,
             page_table, norm_mix, norm_ffn, norm_ple, w_in, b_forget, q_norm, k_norm, w_pool, pool_scale,
             w_attn_out, conv_w, conv_b, conv_ln_g, conv_ln_b, w_conv_out, w_out, dense_w_gate, dense_w_up,
             dense_w_down, router_w, moe_w_gate, moe_w_up, moe_w_down, w_ple, w_ple_gate):
    B, L, D = x_prompt.shape
    DB, S, _ = x_sample.shape
    depth = norm_mix.shape[0]
    H = b_forget.shape[1]
    A = H * HEAD_DIM
    CP = w_pool.shape[1] * w_pool.shape[2]
    CC = conv_w.shape[2]
    page = cache_k.shape[2]
    past_len = page_table.shape[1] * page
    P, NS = B * L, DB * S
    c_q, c_k, c_v, c_f = CP, CP + A, CP + 2 * A, CP + 3 * A
    c_glu = c_f + H
    assert c_f == 7 * IN_TILE and CP == IN_TILE and A == 2 * IN_TILE and P % ROW_TILE == 0

    vec3 = lambda a: a.reshape(a.shape[0], 1, a.shape[1])
    norm_mix3, norm_ffn3, norm_ple3 = vec3(norm_mix), vec3(norm_ffn), vec3(norm_ple)
    b_forget3, pool_scale3 = vec3(b_forget), vec3(pool_scale)
    conv_b3, ln_g3, ln_b3 = vec3(conv_b), vec3(conv_ln_g), vec3(conv_ln_b)
    w_f = w_in[:, :, c_f:c_glu]
    w_rest32 = w_in[:, :, c_glu:]
    w_rest = w_rest32.astype(BF16)
    cache_lf2 = cache_logf.reshape(depth, cache_logf.shape[1], page * H // LANES, LANES)
    dense4 = [w.reshape(w.shape[0], 1, *w.shape[1:]) for w in (dense_w_gate, dense_w_up, dense_w_down)]
    zero_pool = jnp.zeros((B, POOL_STATE, CP), F32)
    zero_conv = jnp.zeros((B, CONV_STATE, CC), F32)

    n_dense_tiles = P // ROW_TILE
    dense_te = jnp.zeros((n_dense_tiles,), jnp.int32)
    dense_tr = jnp.full((n_dense_tiles,), ROW_TILE, jnp.int32)
    n_moe_tiles = (TOP_K * P) // ROW_TILE + N_EXPERTS
    one_te = jnp.zeros((1,), jnp.int32)
    one_tr = jnp.full((1,), NS, jnp.int32)
    expert_ids = jnp.arange(N_EXPERTS, dtype=jnp.int32)

    xp = x_prompt.reshape(P, D)
    xs = x_sample.reshape(NS, D)
    outs = [[] for _ in range(10)]
    for i in range(depth):
        j = i // 2

        h = rms_norm_rows(xp, norm_mix3, i)
        z1 = project_pool_qkv(h, w_in, i, q_norm, k_norm)
        lf = project_logf(h, w_f, i, b_forget3)
        u = project_glu(h, w_rest, i, CC)
        gates = project_gates(h, w_rest, i, 2 * CC, 3 * D)
        ya, pool_p = pool_mixer_rows(z1, B, L, 1, zero_pool, w_pool, pool_scale3, i, 0, D)
        yc, conv_p = conv_module_rows(u, B, L, 1, zero_conv, conv_w, conv_b3, ln_g3, ln_b3, w_conv_out, i)
        c = cumsum_rows(lf, B, L).reshape(B, L, H // 2, 2)
        o = prompt_attention(z1, c.transpose(0, 2, 1, 3), c.transpose(0, 2, 3, 1), B, L, H, c_q, c_k, c_v)
        merged = attn_out_merge(o, w_attn_out, i, gates, ya, yc)
        xp = out_proj_residual(merged, w_out, i, xp)
        if i % 2 == 0:
            h2 = rms_norm_rows(xp, norm_ffn3, i)
            y = ffn_tiles(h2, dense_te, dense_tr, *dense4, j)
            xp, h3 = add_norm_rows(xp, y, norm_ple3, i)
        else:
            h2f, idx, wts = route_rows(xp, norm_ffn3, i, router_w, j)
            pos, src, te, tr = moe_dispatch(idx, ROW_TILE, n_moe_tiles)
            y_sorted = ffn_tiles(gather_rows(h2f, src), te, tr, moe_w_gate, moe_w_up, moe_w_down, j)
            xp, h3 = combine_rows(y_sorted, pos, wts, xp, norm_ple3, i)
        xp = ple_residual(h3, w_ple_gate, p_prompt[i].reshape(P, -1), w_ple, i, xp)

        hs = rms_norm_rows(xs, norm_mix3, i, out_dtype=F32)
        z1s = project_pool_qkv(hs, w_in, i, q_norm, k_norm, precise=True)
        lfs = project_logf(hs, w_f, i, b_forget3, precise=True)
        us = project_glu(hs, w_rest32, i, CC, precise=True)
        gates_s = project_gates(hs, w_rest32, i, 2 * CC, 3 * D, precise=True)
        ya_s, pool_s = pool_mixer_rows(z1s, DB, S, DB, state_pool[i], w_pool, pool_scale3, i, past_len, D,
                                       precise=True)
        yc_s, conv_s = conv_module_rows(us, DB, S, DB, state_conv[i], conv_w, conv_b3, ln_g3, ln_b3,
                                        w_conv_out, i, precise=True)
        rows3 = lambda a: a.reshape(DB, S * H, HEAD_DIM)
        lfn = jnp.pad(lfs.reshape(DB, 1, S * H), ((0, 0), (0, 0), (0, LANES - S * H)))
        o_s = sample_attention(rows3(z1s[:, c_q:c_k]), rows3(z1s[:, c_k:c_v]), rows3(z1s[:, c_v:c_f]), lfn,
                               cache_k, cache_v, cache_lf2, page_table, i)
        merged_s = attn_out_merge(o_s.reshape(NS, A), w_attn_out, i, gates_s, ya_s, yc_s, precise=True)
        xs = out_proj_residual(merged_s, w_out, i, xs, precise=True)
        if i % 2 == 0:
            h2s = rms_norm_rows(xs, norm_ffn3, i, out_dtype=F32)
            ys = ffn_tiles(h2s, one_te, one_tr, *dense4, j, precise=True)
            xs, h3s = add_norm_rows(xs, ys, norm_ple3, i, out_dtype=F32)
        else:
            h2s, idx_s, wts_s = route_rows(xs, norm_ffn3, i, router_w, j)
            hit = jnp.any(idx_s.reshape(-1)[None, :] == expert_ids[:, None], axis=1)
            ys = ffn_tiles(h2s, expert_ids, jnp.where(hit, NS, 0).astype(jnp.int32), moe_w_gate, moe_w_up,
                           moe_w_down, j, precise=True)
            xs, h3s = combine_experts(ys, idx_s, wts_s, xs, norm_ple3, i)
        xs = ple_residual(h3s, w_ple_gate, p_sample[i].reshape(NS, -1), w_ple, i, xs, precise=True)

        heads = lambda a, n, l: a.reshape(n, l, H, HEAD_DIM)
        for lst, val in zip(outs, (
                heads(z1[:, c_k:c_v], B, L), heads(z1[:, c_v:c_f], B, L), lf.reshape(B, L, H), pool_p, conv_p,
                heads(z1s[:, c_k:c_v], DB, S), heads(z1s[:, c_v:c_f], DB, S), lfs.reshape(DB, S, H), pool_s,
                conv_s)):
            lst.append(val)
    return (xp.reshape(B, L, D), xs.reshape(DB, S, D), *(jnp.stack(l) for l in outs))


def kernel(x_prompt, x_sample, p_prompt, p_sample, cache_k, cache_v, cache_logf, state_pool, state_conv, page_table, norm_mix, norm_ffn, norm_ple, w_in, b_forget, q_norm, k_norm, w_pool, pool_scale, w_attn_out, conv_w, conv_b, conv_ln_g, conv_ln_b, w_conv_out, w_out, dense_w_gate, dense_w_up, dense_w_down, router_w, moe_w_gate, moe_w_up, moe_w_down, w_ple, w_ple_gate):
    return _forward(x_prompt, x_sample, p_prompt, p_sample, cache_k, cache_v, cache_logf, state_pool, state_conv,
                    page_table, norm_mix, norm_ffn, norm_ple, w_in, b_forget, q_norm, k_norm, w_pool, pool_scale,
                    w_attn_out, conv_w, conv_b, conv_ln_g, conv_ln_b, w_conv_out, w_out, dense_w_gate,
                    dense_w_up, dense_w_down, router_w, moe_w_gate, moe_w_up, moe_w_down, w_ple, w_ple_gate)
```

```python
import functools

import jax
import jax.numpy as jnp
from jax import lax
from jax.experimental import pallas as pl
from jax.experimental.pallas import tpu as pltpu

F32 = jnp.float32
BF16 = jnp.bfloat16

HEAD_DIM = 64
POOL_WINDOWS = (2, 4, 8, 16)
POOL_GROUP = 128
POOL_STATE = max(POOL_WINDOWS) - 1
CONV_WIDTH = 31
CONV_STATE = CONV_WIDTH - 1
N_EXPERTS = 8
TOP_K = 2
RMS_EPS = 1e-6
LN_EPS = 1e-5
ATTN_SCALE = HEAD_DIM ** -0.5
LOG2_E = 1.4426950408889634
LANES = 128

V7X_VMEM_LIMIT_BYTES = 56 * 1024 * 1024
ROW_TILE = 1024
MOE_SUB_ROWS = 256
DENSE_SUB_ROWS = 512


def _params(sem):
    return pltpu.CompilerParams(dimension_semantics=sem, vmem_limit_bytes=V7X_VMEM_LIMIT_BYTES)


def _row_branches(i, n_full, bm, rem, do):
    if rem == 0:
        do(bm)
        return

    @pl.when(i < n_full)
    def _():
        do(bm)

    @pl.when(i == n_full)
    def _():
        do(rem)


def _rms(x, g):
    return x * lax.rsqrt(jnp.mean(x * x, axis=-1, keepdims=True) + RMS_EPS) * g


_NN = (((1,), (0,)), ((), ()))
_NT = (((1,), (1,)), ((), ()))


def _split(x):
    hi = x.astype(BF16)
    return hi, (x - hi.astype(F32)).astype(BF16)


def _dot3_parts(xh, xl, wh, wl, dims=_NN):
    m = xh.shape[0]
    both = lax.dot_general(jnp.concatenate([xh, xl], axis=0), wh, dims, preferred_element_type=F32)
    return both[:m] + (both[m:] + lax.dot_general(xh, wl, dims, preferred_element_type=F32))


def _dot3(x, w, dims=_NN):
    return _dot3_parts(*_split(x), *_split(w), dims)


def _norm_body(x_ref, g_ref, h_ref, *, bm, n_full, rem):
    def do(rows):
        h_ref[:rows] = _rms(x_ref[:rows], g_ref[...]).astype(h_ref.dtype)

    _row_branches(pl.program_id(0), n_full, bm, rem, do)


def rms_norm_rows(x, g3, layer, out_dtype=BF16, bm=512):
    T, D = x.shape
    bm = min(bm, T)
    n_full, rem = divmod(T, bm)
    return pl.pallas_call(
        functools.partial(_norm_body, bm=bm, n_full=n_full, rem=rem),
        out_shape=jax.ShapeDtypeStruct((T, D), out_dtype),
        grid=(pl.cdiv(T, bm),),
        in_specs=[pl.BlockSpec((bm, D), lambda i: (i, 0)),
                  pl.BlockSpec((None, 1, D), lambda i: (layer, 0, 0))],
        out_specs=pl.BlockSpec((bm, D), lambda i: (i, 0)),
        compiler_params=_params(("arbitrary",)),
        name="rms_norm_rows",
    )(x, g3)


def _add_norm_body(x_ref, y_ref, g_ref, x2_ref, h_ref, *, bm, n_full, rem):
    def do(rows):
        x2 = x_ref[:rows] + y_ref[:rows]
        x2_ref[:rows] = x2
        h_ref[:rows] = _rms(x2, g_ref[...]).astype(h_ref.dtype)

    _row_branches(pl.program_id(0), n_full, bm, rem, do)


def add_norm_rows(x, y, g3, layer, out_dtype=BF16, bm=512):
    T, D = x.shape
    bm = min(bm, T)
    n_full, rem = divmod(T, bm)
    spec = pl.BlockSpec((bm, D), lambda i: (i, 0))
    return pl.pallas_call(
        functools.partial(_add_norm_body, bm=bm, n_full=n_full, rem=rem),
        out_shape=(jax.ShapeDtypeStruct((T, D), F32), jax.ShapeDtypeStruct((T, D), out_dtype)),
        grid=(pl.cdiv(T, bm),),
        in_specs=[spec, spec, pl.BlockSpec((None, 1, D), lambda i: (layer, 0, 0))],
        out_specs=(spec, spec),
        compiler_params=_params(("arbitrary",)),
        name="add_norm_rows",
    )(x, y, g3)


def _mm_body(*refs, n_pairs, n_extra, n_out, cast_w, precise, epilogue, bm, n_full, rem):
    xs = refs[:n_pairs]
    ws = refs[n_pairs:2 * n_pairs]
    extras = refs[2 * n_pairs:2 * n_pairs + n_extra]
    outs = refs[2 * n_pairs + n_extra:2 * n_pairs + n_extra + n_out]
    wbs = refs[2 * n_pairs + n_extra + n_out:]
    j = pl.program_id(0)
    i = pl.program_id(1)

    @pl.when(i == 0)
    def _():
        k = 0
        for p in range(n_pairs):
            if cast_w[p]:
                wbs[k][...] = ws[p][...].astype(BF16)
                k += 1

    def do(rows):
        accs = []
        k = 0
        for p in range(n_pairs):
            x = xs[p][:rows]
            if precise:
                accs.append(_dot3(x, ws[p][...]))
                continue
            if x.dtype != BF16:
                x = x.astype(BF16)
            if cast_w[p]:
                w = wbs[k][...]
                k += 1
            else:
                w = ws[p][...]
            accs.append(jnp.dot(x, w, preferred_element_type=F32))
        epilogue(accs, rows, j, extras, outs)

    _row_branches(i, n_full, bm, rem, do)


def matmul_rows(pairs, extras, outs, epilogue, *, tn, n_tiles, name, bm, precise=False):
    T = pairs[0][0].shape[0]
    n_full, rem = divmod(T, bm)
    in_specs, args, cast_w, scratch = [], [], [], []
    for x, _, _, _ in pairs:
        in_specs.append(pl.BlockSpec((bm, x.shape[1]), lambda j, i: (i, 0)))
        args.append(x)
    for x, w, lead, off in pairs:
        K = x.shape[1]
        blk = (None,) * len(lead) + (K, tn)
        in_specs.append(pl.BlockSpec(blk, lambda j, i, lead=lead, off=off: (*lead, 0, j + off)))
        args.append(w)
        assert not precise or (w.dtype == F32 and x.dtype == F32)
        cast_w.append(w.dtype != BF16 and not precise)
        if cast_w[-1]:
            scratch.append(pltpu.VMEM((K, tn), BF16))
    for a, blk, imap in extras:
        in_specs.append(pl.BlockSpec(blk, imap))
        args.append(a)
    return pl.pallas_call(
        functools.partial(_mm_body, n_pairs=len(pairs), n_extra=len(extras), n_out=len(outs),
                          cast_w=tuple(cast_w), precise=precise, epilogue=epilogue, bm=bm, n_full=n_full,
                          rem=rem),
        out_shape=tuple(o[0] for o in outs),
        grid=(n_tiles, pl.cdiv(T, bm)),
        in_specs=in_specs,
        out_specs=tuple(pl.BlockSpec(o[1], o[2]) for o in outs),
        scratch_shapes=scratch,
        compiler_params=_params(("arbitrary", "arbitrary")),
        name=name,
    )(*args)


def _tile_spec(bm, tn, off=0):
    return (bm, tn), (lambda j, i, off=off: (i, j + off))


IN_TILE = 512


def _epi_pool_qkv(accs, rows, j, extras, outs):
    (acc,) = accs
    bd_ref, g_ref = extras
    (o_ref,) = outs
    is_head_norm = (j >= 1) & (j <= 4)

    @pl.when(is_head_norm)
    def _():
        hi, lo = _split(acc * acc)
        ss = (jnp.dot(hi, bd_ref[...], preferred_element_type=F32)
              + jnp.dot(lo, bd_ref[...], preferred_element_type=F32))
        o_ref[:rows] = acc * lax.rsqrt(ss * (1.0 / HEAD_DIM) + RMS_EPS) * g_ref[...]

    @pl.when(jnp.logical_not(is_head_norm))
    def _():
        o_ref[:rows] = acc


def project_pool_qkv(h, w_in, layer, q_norm, k_norm, precise=False):
    T = h.shape[0]
    tn = IN_TILE
    bm = min(ROW_TILE, T)
    n_tiles = 7
    heads_per_tile = tn // HEAD_DIM
    ones = jnp.ones((tn,), F32)
    gq = jnp.tile(q_norm[layer], heads_per_tile)
    gk = jnp.tile(k_norm[layer], heads_per_tile)
    gains = jnp.stack([ones, gq, gq, gk, gk, ones, ones]).reshape(n_tiles, 1, tn)
    grp = jnp.arange(tn) // HEAD_DIM
    bd = (grp[:, None] == grp[None, :]).astype(BF16)
    blk, imap = _tile_spec(bm, tn)
    (z1,) = matmul_rows(
        [(h, w_in, (layer,), 0)],
        [(bd, (tn, tn), lambda j, i: (0, 0)), (gains, (None, 1, tn), lambda j, i: (j, 0, 0))],
        [(jax.ShapeDtypeStruct((T, n_tiles * tn), F32), blk, imap)],
        _epi_pool_qkv, tn=tn, n_tiles=n_tiles, name="project_pool_qkv", bm=bm, precise=precise)
    return z1


def _epi_logf(accs, rows, j, extras, outs):
    (acc,) = accs
    (b_ref,) = extras
    (o_ref,) = outs
    o_ref[:rows] = jax.nn.log_sigmoid(acc + b_ref[...])


def project_logf(h, w_f, layer, b_forget3, precise=False):
    T = h.shape[0]
    H = w_f.shape[-1]
    bm = min(ROW_TILE, T)
    (lf,) = matmul_rows(
        [(h, w_f, (layer,), 0)],
        [(b_forget3, (None, 1, H), lambda j, i: (layer, 0, 0))],
        [(jax.ShapeDtypeStruct((T, H), F32), (bm, H), lambda j, i: (i, 0))],
        _epi_logf, tn=H, n_tiles=1, name="project_logf", bm=bm, precise=precise)
    return lf


def _epi_glu(accs, rows, j, extras, outs):
    a, b = accs
    (o_ref,) = outs
    o_ref[:rows] = a * jax.nn.sigmoid(b)


def project_glu(h, w_rest, layer, conv_dim, precise=False):
    T = h.shape[0]
    bm = min(ROW_TILE, T)
    (u,) = matmul_rows(
        [(h, w_rest, (layer,), 0), (h, w_rest, (layer,), 1)],
        [],
        [(jax.ShapeDtypeStruct((T, conv_dim), F32), (bm, conv_dim), lambda j, i: (i, 0))],
        _epi_glu, tn=conv_dim, n_tiles=1, name="project_glu", bm=bm, precise=precise)
    return u


def _epi_sigmoid(accs, rows, j, extras, outs):
    (acc,) = accs
    (o_ref,) = outs
    o_ref[:rows] = jax.nn.sigmoid(acc)


def project_gates(h, w_rest, layer, col0, n_cols, tn=1024, precise=False):
    T = h.shape[0]
    bm = min(ROW_TILE, T)
    blk, imap = _tile_spec(bm, tn)
    (g,) = matmul_rows(
        [(h, w_rest, (layer,), col0 // tn)],
        [],
        [(jax.ShapeDtypeStruct((T, n_cols), F32), blk, imap)],
        _epi_sigmoid, tn=tn, n_tiles=n_cols // tn, name="project_gates", bm=bm, precise=precise)
    return g


def _epi_merge(accs, rows, j, extras, outs):
    (yb,) = accs
    g0, g1, g2, ya, yc = extras
    (o_ref,) = outs
    merged = g0[:rows] * ya[:rows] + g1[:rows] * yb + g2[:rows] * yc[:rows]
    o_ref[:rows] = merged.astype(o_ref.dtype)


def attn_out_merge(o, w_attn_out, layer, gates, ya, yc, tn=1024, precise=False):
    T = o.shape[0]
    D = w_attn_out.shape[-1]
    nt = D // tn
    bm = min(ROW_TILE // 2, T)
    blk, imap = _tile_spec(bm, tn)
    (m,) = matmul_rows(
        [(o, w_attn_out, (layer,), 0)],
        [(gates, *_tile_spec(bm, tn, 0)), (gates, *_tile_spec(bm, tn, nt)),
         (gates, *_tile_spec(bm, tn, 2 * nt)), (ya, blk, imap), (yc, blk, imap)],
        [(jax.ShapeDtypeStruct((T, D), F32 if precise else BF16), blk, imap)],
        _epi_merge, tn=tn, n_tiles=nt, name="attn_out_merge", bm=bm, precise=precise)
    return m


def _epi_residual(accs, rows, j, extras, outs):
    (acc,) = accs
    (x_ref,) = extras
    (o_ref,) = outs
    o_ref[:rows] = x_ref[:rows] + acc


def out_proj_residual(merged, w_out, layer, x, tn=1024, precise=False):
    T, D = x.shape
    bm = min(ROW_TILE, T)
    blk, imap = _tile_spec(bm, tn)
    (x2,) = matmul_rows(
        [(merged, w_out, (layer,), 0)],
        [(x, blk, imap)],
        [(jax.ShapeDtypeStruct((T, D), F32), blk, imap)],
        _epi_residual, tn=tn, n_tiles=D // tn, name="out_proj_residual", bm=bm, precise=precise)
    return x2


def _epi_ple(accs, rows, j, extras, outs):
    gate_logit, emb = accs
    (x_ref,) = extras
    (o_ref,) = outs
    o_ref[:rows] = x_ref[:rows] + jax.nn.sigmoid(gate_logit) * emb


def ple_residual(h, w_ple_gate, p, w_ple, layer, x, tn=1024, precise=False):
    T, D = x.shape
    bm = min(ROW_TILE // 2, T)
    blk, imap = _tile_spec(bm, tn)
    (x2,) = matmul_rows(
        [(h, w_ple_gate, (layer,), 0), (p, w_ple, (layer,), 0)],
        [(x, blk, imap)],
        [(jax.ShapeDtypeStruct((T, D), F32), blk, imap)],
        _epi_ple, tn=tn, n_tiles=D // tn, name="ple_residual", bm=bm, precise=precise)
    return x2


POOL_PAD = 16
CONV_PAD = 32


def _pool_body(u_ref, pre_ref, w_ref, sc_ref, ya_ref, st_ref, ext_ref, *, nb, L, R, offset, precise):
    C = ext_ref.shape[-1]
    c = pl.program_id(1)

    @pl.when(c == 0)
    def _():
        for lb in range(nb):
            ext_ref[lb, 0:POOL_PAD - POOL_STATE, :] = jnp.zeros((POOL_PAD - POOL_STATE, C), F32)
            ext_ref[lb, POOL_PAD - POOL_STATE:POOL_PAD, :] = pre_ref[lb]
            ext_ref[lb, POOL_PAD:POOL_PAD + L, :] = u_ref[lb * L:(lb + 1) * L, :]
            st_ref[lb] = ext_ref[lb, POOL_PAD + L - POOL_STATE:POOL_PAD + L, :]

    t0 = 0 if L == R else pl.multiple_of(c * R, R)
    pos = offset + t0 + lax.broadcasted_iota(jnp.int32, (R, 1), 0)
    wins = [ext_ref[lb, pl.ds(t0, R + POOL_PAD), :] for lb in range(nb)]
    for g, w in enumerate(POOL_WINDOWS):
        cnt = jnp.minimum(pos + 1, w).astype(F32)
        rs = []
        for lb in range(nb):
            wg = wins[lb][:, g * POOL_GROUP:(g + 1) * POOL_GROUP]
            cur = wg[POOL_PAD:POOL_PAD + R]
            acc = cur
            for i in range(1, w):
                acc = acc + wg[POOL_PAD - i:POOL_PAD - i + R]
            rs.append(acc / cnt - cur)
        r = rs[0] if nb == 1 else jnp.concatenate(rs, axis=0)
        if precise:
            y = _dot3(r, w_ref[g])
        else:
            y = jnp.dot(r.astype(BF16), w_ref[g].astype(BF16), preferred_element_type=F32)
        og = y.shape[1]
        ya_ref[:, g * og:(g + 1) * og] = y * sc_ref[:, g * og:(g + 1) * og]


def pool_mixer_rows(z1, nseq, L, nb, prefix, w_pool, pool_scale3, layer, offset, d_model, precise=False):
    C = len(POOL_WINDOWS) * POOL_GROUP
    R = min(L, 512)
    n_chunks = L // R
    return pl.pallas_call(
        functools.partial(_pool_body, nb=nb, L=L, R=R, offset=offset, precise=precise),
        out_shape=(jax.ShapeDtypeStruct((nseq * L, d_model), F32),
                   jax.ShapeDtypeStruct((nseq, POOL_STATE, C), F32)),
        grid=(nseq // nb, n_chunks),
        in_specs=[pl.BlockSpec((nb * L, C), lambda s, c: (s, 0)),
                  pl.BlockSpec((nb, POOL_STATE, C), lambda s, c: (s, 0, 0)),
                  pl.BlockSpec((None,) + w_pool.shape[1:], lambda s, c: (layer, 0, 0, 0)),
                  pl.BlockSpec((None, 1, d_model), lambda s, c: (layer, 0, 0))],
        out_specs=(pl.BlockSpec((nb * R, d_model), lambda s, c: (s * n_chunks + c, 0)),
                   pl.BlockSpec((nb, POOL_STATE, C), lambda s, c: (s, 0, 0))),
        scratch_shapes=[pltpu.VMEM((nb, POOL_PAD + L, C), F32)],
        compiler_params=_params(("arbitrary", "arbitrary")),
        name="pool_mixer_rows",
    )(z1, prefix, w_pool, pool_scale3)


def _conv_body(u_ref, pre_ref, cw_ref, cb_ref, lg_ref, lb_ref, wo_ref, yc_ref, st_ref, ext_ref, *, nb, L, R,
               precise):
    C = ext_ref.shape[-1]
    c = pl.program_id(1)

    @pl.when(c == 0)
    def _():
        for lb in range(nb):
            ext_ref[lb, 0:CONV_PAD - CONV_STATE, :] = jnp.zeros((CONV_PAD - CONV_STATE, C), F32)
            ext_ref[lb, CONV_PAD - CONV_STATE:CONV_PAD, :] = pre_ref[lb]
            ext_ref[lb, CONV_PAD:CONV_PAD + L, :] = u_ref[lb * L:(lb + 1) * L, :]
            st_ref[lb] = ext_ref[lb, CONV_PAD + L - CONV_STATE:CONV_PAD + L, :]

    t0 = 0 if L == R else pl.multiple_of(c * R, R)
    ys = []
    for lb in range(nb):
        win = ext_ref[lb, pl.ds(t0, R + CONV_PAD), :]
        acc = jnp.zeros((R, C), F32)
        for j in range(CONV_WIDTH):
            s0 = CONV_PAD - CONV_STATE + j
            acc = acc + win[s0:s0 + R] * cw_ref[j:j + 1, :]
        ys.append(acc)
    y = (ys[0] if nb == 1 else jnp.concatenate(ys, axis=0)) + cb_ref[...]
    mu = jnp.mean(y, axis=-1, keepdims=True)
    yc = y - mu
    yn = yc * lax.rsqrt(jnp.mean(yc * yc, axis=-1, keepdims=True) + LN_EPS) * lg_ref[...] + lb_ref[...]
    act = yn * jax.nn.sigmoid(yn)
    if precise:
        yc_ref[...] = _dot3(act, wo_ref[...])
    else:
        yc_ref[...] = jnp.dot(act.astype(BF16), wo_ref[...].astype(BF16), preferred_element_type=F32)


def conv_module_rows(u, nseq, L, nb, prefix, conv_w, conv_b3, ln_g3, ln_b3, w_conv_out, layer, precise=False):
    C = u.shape[1]
    D = w_conv_out.shape[-1]
    R = min(L, 256)
    n_chunks = L // R
    vec = lambda a: pl.BlockSpec((None, 1, a.shape[-1]), lambda s, c: (layer, 0, 0))
    return pl.pallas_call(
        functools.partial(_conv_body, nb=nb, L=L, R=R, precise=precise),
        out_shape=(jax.ShapeDtypeStruct((nseq * L, D), F32), jax.ShapeDtypeStruct((nseq, CONV_STATE, C), F32)),
        grid=(nseq // nb, n_chunks),
        in_specs=[pl.BlockSpec((nb * L, C), lambda s, c: (s, 0)),
                  pl.BlockSpec((nb, CONV_STATE, C), lambda s, c: (s, 0, 0)),
                  pl.BlockSpec((None, CONV_WIDTH, C), lambda s, c: (layer, 0, 0)),
                  vec(conv_b3), vec(ln_g3), vec(ln_b3),
                  pl.BlockSpec((None, C, D), lambda s, c: (layer, 0, 0))],
        out_specs=(pl.BlockSpec((nb * R, D), lambda s, c: (s * n_chunks + c, 0)),
                   pl.BlockSpec((nb, CONV_STATE, C), lambda s, c: (s, 0, 0))),
        scratch_shapes=[pltpu.VMEM((nb, CONV_PAD + L, C), F32)],
        compiler_params=_params(("arbitrary", "arbitrary")),
        name="conv_module_rows",
    )(u, prefix, conv_w, conv_b3, ln_g3, ln_b3, w_conv_out)


CUMSUM_CHUNK = 128


def _cumsum_body(x_ref, o_ref, *, L):
    n = CUMSUM_CHUNK
    tri = (lax.broadcasted_iota(jnp.int32, (n, n), 0) >= lax.broadcasted_iota(jnp.int32, (n, n), 1)).astype(F32)
    carry = jnp.zeros((1, x_ref.shape[-1]), F32)
    for k in range(L // n):
        ck = jnp.dot(tri, x_ref[k * n:(k + 1) * n, :], precision=lax.Precision.HIGHEST,
                     preferred_element_type=F32) + carry
        o_ref[k * n:(k + 1) * n, :] = ck
        carry = ck[n - 1:n, :]


def cumsum_rows(logf, nseq, L):
    H = logf.shape[1]
    return pl.pallas_call(
        functools.partial(_cumsum_body, L=L),
        out_shape=jax.ShapeDtypeStruct((nseq * L, H), F32),
        grid=(nseq,),
        in_specs=[pl.BlockSpec((L, H), lambda b: (b, 0))],
        out_specs=pl.BlockSpec((L, H), lambda b: (b, 0)),
        compiler_params=_params(("arbitrary",)),
        name="cumsum_rows",
    )(logf)


def _flash_body(q_ref, k_ref, v_ref, ck_ref, o_ref, m_ref, l_ref, acc_ref, *, blk):
    qi = pl.program_id(2)
    q2 = q_ref[...] * (ATTN_SCALE * LOG2_E)
    first = lax.broadcasted_iota(jnp.int32, (1, 2 * HEAD_DIM), 1) < HEAD_DIM
    qh = (jnp.where(first, q2, 0.0).astype(BF16), jnp.where(first, 0.0, q2).astype(BF16))
    m_ref[...] = jnp.full(m_ref.shape, -jnp.inf, F32)
    l_ref[...] = jnp.zeros(l_ref.shape, F32)
    acc_ref[...] = jnp.zeros(acc_ref.shape, F32)
    causal = (lax.broadcasted_iota(jnp.int32, (blk, blk), 0) >= lax.broadcasted_iota(jnp.int32, (blk, blk), 1))

    def chunk(kc, masked):
        k0 = pl.multiple_of(kc * blk, blk)
        k2 = k_ref[pl.ds(k0, blk), :].astype(BF16)
        v2 = v_ref[pl.ds(k0, blk), :].astype(BF16)
        ck = ck_ref[:, pl.ds(k0, blk)] * LOG2_E
        for h in range(2):
            s = lax.dot_general(qh[h], k2, _NT, preferred_element_type=F32) - ck[h:h + 1, :]
            if masked:
                s = jnp.where(causal, s, -jnp.inf)
            m_prev = m_ref[h]
            m_new = jnp.maximum(m_prev, jnp.max(s, axis=1, keepdims=True))
            alpha = jnp.exp2(m_prev - m_new)
            p = jnp.exp2(s - m_new)
            l_ref[h] = alpha * l_ref[h] + jnp.sum(p, axis=1, keepdims=True)
            acc_ref[h] = alpha * acc_ref[h] + jnp.dot(p.astype(BF16), v2, preferred_element_type=F32)
            m_ref[h] = m_new

    def body(kc, carry):
        chunk(kc, False)
        return carry

    lax.fori_loop(0, qi, body, 0)
    chunk(qi, True)
    out = jnp.where(first, acc_ref[0] / l_ref[0], acc_ref[1] / l_ref[1])
    o_ref[...] = out.astype(o_ref.dtype)


def prompt_attention(z1, ct_pairs, nseq, L, n_heads, q_col, k_col, v_col):
    blk = min(L, 512)
    nq = L // blk
    lanes = 2 * HEAD_DIM
    qc, kc, vc = q_col // lanes, k_col // lanes, v_col // lanes
    return pl.pallas_call(
        functools.partial(_flash_body, blk=blk),
        out_shape=jax.ShapeDtypeStruct((nseq * L, n_heads * HEAD_DIM), BF16),
        grid=(nseq, n_heads // 2, nq),
        in_specs=[pl.BlockSpec((blk, lanes), lambda b, hp, qi: (b * nq + qi, qc + hp)),
                  pl.BlockSpec((L, lanes), lambda b, hp, qi: (b, kc + hp)),
                  pl.BlockSpec((L, lanes), lambda b, hp, qi: (b, vc + hp)),
                  pl.BlockSpec((None, None, 2, L), lambda b, hp, qi: (b, hp, 0, 0))],
        out_specs=pl.BlockSpec((blk, lanes), lambda b, hp, qi: (b * nq + qi, hp)),
        scratch_shapes=[pltpu.VMEM((2, blk, 1), F32), pltpu.VMEM((2, blk, 1), F32),
                        pltpu.VMEM((2, blk, lanes), F32)],
        compiler_params=_params(("arbitrary", "arbitrary", "arbitrary")),
        name="prompt_attention",
    )(z1, z1, z1, ct_pairs)


DECODE_PAGES_PER_STEP = 8
NEW_PAD = 16


def _decode_body(pt_ref, q_ref, kn_ref, vn_ref, lfn_ref, *rest, PG, H, page):
    k_refs, v_refs, lf_refs = rest[:PG], rest[PG:2 * PG], rest[2 * PG:3 * PG]
    o_ref, m_ref, l_ref, acc_ref, carry_ref, cq_ref = rest[3 * PG:]
    s = pl.program_id(1)
    TH, HD = q_ref.shape
    S = TH // H
    sh_h = H.bit_length() - 1
    qh, ql = _split(q_ref[...] * ATTN_SCALE)
    row = lax.broadcasted_iota(jnp.int32, (TH, 1), 0)
    rows_of = lambda a: jnp.concatenate([a] * S, axis=0)

    @pl.when(s == 0)
    def _():
        upper = (lax.broadcasted_iota(jnp.int32, (NEW_PAD, NEW_PAD), 0)
                 <= lax.broadcasted_iota(jnp.int32, (NEW_PAD, NEW_PAD), 1)).astype(F32)
        cn = rows_of(jnp.dot(lfn_ref[...], upper, precision=lax.Precision.HIGHEST,
                             preferred_element_type=F32))
        jj = lax.broadcasted_iota(jnp.int32, (TH, NEW_PAD), 1)
        tt = row >> sh_h
        cq = jnp.sum(jnp.where(jj == tt, cn, 0.0), axis=1, keepdims=True)
        cq_ref[...] = cq
        sc = _dot3_parts(qh, ql, *_split(kn_ref[...]), _NT) + (cq - cn)
        sc = jnp.where(jj <= tt, sc, -jnp.inf)
        m = jnp.max(sc, axis=1, keepdims=True)
        p = jnp.exp(sc - m)
        m_ref[...] = m
        l_ref[...] = jnp.sum(p, axis=1, keepdims=True)
        acc_ref[...] = _dot3(p, vn_ref[...])
        carry_ref[...] = jnp.zeros(carry_ref.shape, F32)

    lane = lax.broadcasted_iota(jnp.int32, (1, page), 1)
    carry = carry_ref[...]
    biases = []
    for r in range(PG):
        lf = lf_refs[r][...]
        after = jnp.where(lane < page - 1, pltpu.roll(lf, page - 1, axis=1), 0.0)
        sh = 1
        while sh < page:
            after = after + jnp.where(lane < page - sh, pltpu.roll(after, page - sh, axis=1), 0.0)
            sh *= 2
        biases.append(rows_of(after) + carry)
        carry = carry + rows_of(jnp.sum(lf, axis=1, keepdims=True))
    carry_ref[...] = carry

    kT = jnp.concatenate([k_refs[r][...].reshape(HD, page) for r in range(PG)], axis=1)
    sc = _dot3_parts(qh, ql, *_split(kT)) + (jnp.concatenate(biases, axis=1) + cq_ref[...])
    m_prev = m_ref[...]
    m_new = jnp.maximum(m_prev, jnp.max(sc, axis=1, keepdims=True))
    alpha = jnp.exp(m_prev - m_new)
    p = jnp.exp(sc - m_new)
    l_new = alpha * l_ref[...] + jnp.sum(p, axis=1, keepdims=True)
    vT = jnp.concatenate([v_refs[r][...].reshape(HD, page) for r in range(PG)], axis=1)
    acc = alpha * acc_ref[...] + _dot3(p, vT, _NT)
    m_ref[...] = m_new
    l_ref[...] = l_new
    acc_ref[...] = acc

    @pl.when(s == pl.num_programs(1) - 1)
    def _():
        own_head = (lax.broadcasted_iota(jnp.int32, (TH, HD), 1) >> (HEAD_DIM.bit_length() - 1)) == (row & (H - 1))
        od = jnp.where(own_head, acc / l_new, 0.0)
        o_ref[...] = jnp.concatenate([jnp.sum(od[t * H:(t + 1) * H], axis=0, keepdims=True) for t in range(S)],
                                     axis=0)


def sample_attention(q_bd, kn, vn, lfn, cache_kt, cache_vt, cache_lft, page_table, layer):
    B, TH, HD = q_bd.shape
    _, _, H, Dh, page = cache_kt.shape
    S = TH // H
    n_pages = page_table.shape[1]
    PG = DECODE_PAGES_PER_STEP
    assert n_pages % PG == 0 and H & (H - 1) == 0 and Dh == HEAD_DIM and page % LANES == 0 and S <= NEW_PAD

    def page_map(r):
        return lambda b, s, pt: (layer, pt[b, n_pages - 1 - (s * PG + r)], 0, 0, 0)

    def lf_map(r):
        return lambda b, s, pt: (layer, pt[b, n_pages - 1 - (s * PG + r)], 0, 0)

    per_batch = lambda a: pl.BlockSpec((None,) + a.shape[1:], lambda b, s, pt: (b, 0, 0))
    in_specs = [per_batch(q_bd), per_batch(kn), per_batch(vn), per_batch(lfn)]
    in_specs += [pl.BlockSpec((None, None, H, Dh, page), page_map(r)) for r in range(PG)]
    in_specs += [pl.BlockSpec((None, None, H, Dh, page), page_map(r)) for r in range(PG)]
    in_specs += [pl.BlockSpec((None, None, H, page), lf_map(r)) for r in range(PG)]
    return pl.pallas_call(
        functools.partial(_decode_body, PG=PG, H=H, page=page),
        out_shape=jax.ShapeDtypeStruct((B, S, HD), F32),
        grid_spec=pltpu.PrefetchScalarGridSpec(
            num_scalar_prefetch=1,
            grid=(B, n_pages // PG),
            in_specs=in_specs,
            out_specs=pl.BlockSpec((None, S, HD), lambda b, s, pt: (b, 0, 0)),
            scratch_shapes=[pltpu.VMEM((TH, 1), F32), pltpu.VMEM((TH, 1), F32), pltpu.VMEM((TH, HD), F32),
                            pltpu.VMEM((TH, 1), F32), pltpu.VMEM((TH, 1), F32)]),
        compiler_params=_params(("arbitrary", "arbitrary")),
        name="sample_attention",
    )(page_table, q_bd, kn, vn, lfn, *([cache_kt] * PG), *([cache_vt] * PG), *([cache_lft] * PG))


FFN_CHUNK = 512


def _ffn_body(te_ref, tr_ref, x_ref, wg_ref, wu_ref, wd_ref, o_ref, *scratch, bm, sub, precise):
    s = pl.program_id(0)
    f = pl.program_id(1)
    rows_valid = tr_ref[s]

    def accumulate(r0, n, y):
        @pl.when(f == 0)
        def _():
            o_ref[r0:r0 + n] = y

        @pl.when(f > 0)
        def _():
            o_ref[r0:r0 + n] += y

    if precise:
        @pl.when(rows_valid > 0)
        def _():
            x = x_ref[...]
            g = _dot3(x, wg_ref[...])
            u = _dot3(x, wu_ref[...])
            accumulate(0, bm, _dot3(g * jax.nn.sigmoid(g) * u, wd_ref[...]))
    else:
        wgb, wub, wdb = scratch

        def block(r0, n):
            x = x_ref[r0:r0 + n]
            g = jnp.dot(x, wgb[...], preferred_element_type=F32)
            u = jnp.dot(x, wub[...], preferred_element_type=F32)
            a = (g * jax.nn.sigmoid(g) * u).astype(BF16)
            accumulate(r0, n, jnp.dot(a, wdb[...], preferred_element_type=F32))

        @pl.when(rows_valid > 0)
        def _():
            wgb[...] = wg_ref[...].astype(BF16)
            wub[...] = wu_ref[...].astype(BF16)
            wdb[...] = wd_ref[...].astype(BF16)
            block(0, sub)
            for sb in range(1, bm // sub):
                @pl.when(sb * sub < rows_valid)
                def _():
                    block(sb * sub, sub)

                @pl.when((sb * sub >= rows_valid) & (f == 0))
                def _():
                    o_ref[sb * sub:(sb + 1) * sub] = jnp.zeros((sub, o_ref.shape[1]), F32)

    @pl.when((rows_valid == 0) & (f == 0))
    def _():
        o_ref[...] = jnp.zeros(o_ref.shape, F32)


def ffn_tiles(x, tile_expert, tile_rows, w_gate, w_up, w_down, lw, bm=ROW_TILE, sub_rows=MOE_SUB_ROWS,
              precise=False):
    D = x.shape[1]
    F = w_gate.shape[-1]
    tf = FFN_CHUNK
    nf = F // tf
    bm = min(bm, x.shape[0])
    n_tiles = tile_expert.shape[0]
    assert precise or x.shape[0] == n_tiles * bm

    def f_eff(s, f, tr):
        return jnp.where(tr[s] > 0, f, nf - 1)

    x_spec = (pl.BlockSpec((bm, D), lambda s, f, te, tr: (0, 0)) if precise else
              pl.BlockSpec((bm, D), lambda s, f, te, tr: (s, 0), pipeline_mode=pl.Buffered(1)))
    in_specs = [x_spec,
                pl.BlockSpec((None, None, D, tf), lambda s, f, te, tr: (lw, te[s], 0, f_eff(s, f, tr))),
                pl.BlockSpec((None, None, D, tf), lambda s, f, te, tr: (lw, te[s], 0, f_eff(s, f, tr))),
                pl.BlockSpec((None, None, tf, D), lambda s, f, te, tr: (lw, te[s], f_eff(s, f, tr), 0))]
    scratch = [] if precise else [pltpu.VMEM((D, tf), BF16), pltpu.VMEM((D, tf), BF16),
                                  pltpu.VMEM((tf, D), BF16)]
    return pl.pallas_call(
        functools.partial(_ffn_body, bm=bm, sub=min(sub_rows, bm), precise=precise),
        out_shape=jax.ShapeDtypeStruct((n_tiles * bm, D), F32),
        grid_spec=pltpu.PrefetchScalarGridSpec(
            num_scalar_prefetch=2,
            grid=(n_tiles, nf),
            in_specs=in_specs,
            out_specs=pl.BlockSpec((bm, D), lambda s, f, te, tr: (s, 0), pipeline_mode=pl.Buffered(1)),
            scratch_shapes=scratch),
        compiler_params=_params(("arbitrary", "arbitrary")),
        name="ffn_tiles",
    )(tile_expert, tile_rows, x, w_gate, w_up, w_down)


def _router_body(x_ref, g_ref, wr_ref, h_ref, idx_ref, wt_ref, *, bm, n_full, rem):
    def do(rows):
        h = _rms(x_ref[:rows], g_ref[...])
        h_ref[:rows] = h
        logits = jnp.dot(h, wr_ref[...], precision=lax.Precision.HIGHEST, preferred_element_type=F32)
        E = logits.shape[1]
        lane = lax.broadcasted_iota(jnp.int32, (rows, E), 1)
        m1 = jnp.max(logits, axis=1, keepdims=True)
        i1 = jnp.min(jnp.where(logits == m1, lane, E), axis=1, keepdims=True)
        rest = jnp.where(lane == i1, -jnp.inf, logits)
        m2 = jnp.max(rest, axis=1, keepdims=True)
        i2 = jnp.min(jnp.where(rest == m2, lane, E), axis=1, keepdims=True)
        e = jnp.exp(m2 - m1)
        w1 = 1.0 / (1.0 + e)
        slot = lax.broadcasted_iota(jnp.int32, (rows, TOP_K), 1)
        idx_ref[:rows] = jnp.where(slot == 0, i1, i2)
        wt_ref[:rows] = jnp.where(slot == 0, w1, e * w1)

    _row_branches(pl.program_id(0), n_full, bm, rem, do)


def route_rows(x, g3, layer, router_w, lw, bm=512):
    T, D = x.shape
    E = router_w.shape[-1]
    bm = min(bm, T)
    n_full, rem = divmod(T, bm)
    return pl.pallas_call(
        functools.partial(_router_body, bm=bm, n_full=n_full, rem=rem),
        out_shape=(jax.ShapeDtypeStruct((T, D), F32), jax.ShapeDtypeStruct((T, TOP_K), jnp.int32),
                   jax.ShapeDtypeStruct((T, TOP_K), F32)),
        grid=(pl.cdiv(T, bm),),
        in_specs=[pl.BlockSpec((bm, D), lambda i: (i, 0)),
                  pl.BlockSpec((None, 1, D), lambda i: (layer, 0, 0)),
                  pl.BlockSpec((None, D, E), lambda i: (lw, 0, 0))],
        out_specs=(pl.BlockSpec((bm, D), lambda i: (i, 0)), pl.BlockSpec((bm, TOP_K), lambda i: (i, 0)),
                   pl.BlockSpec((bm, TOP_K), lambda i: (i, 0))),
        compiler_params=_params(("arbitrary",)),
        name="route_rows",
    )(x, g3, router_w)


def moe_dispatch(idx, bm, n_tiles):
    T = idx.shape[0]
    e_flat = idx.reshape(-1)
    onehot = (e_flat[:, None] == jnp.arange(N_EXPERTS, dtype=jnp.int32)[None, :]).astype(jnp.int32)
    incl = jnp.cumsum(onehot, axis=0)
    counts = incl[-1]
    rank = jnp.sum((incl - onehot) * onehot, axis=1)
    tiles_e = (counts + bm - 1) // bm
    tile_end = jnp.cumsum(tiles_e)
    tile_start = tile_end - tiles_e
    pos = (tile_start[e_flat] * bm + rank).astype(jnp.int32)
    src = jnp.zeros((n_tiles * bm,), jnp.int32).at[pos].set(jnp.arange(TOP_K * T, dtype=jnp.int32) // TOP_K)
    tile_ids = jnp.arange(n_tiles, dtype=jnp.int32)
    used = tile_ids < tile_end[-1]
    owner = jnp.sum((tile_ids[:, None] >= tile_end[None, :]).astype(jnp.int32), axis=1)
    owner = jnp.where(used, owner, owner[jnp.maximum(tile_end[-1] - 1, 0)])
    rows = jnp.clip(counts[owner] - (tile_ids - tile_start[owner]) * bm, 0, bm)
    rows = jnp.where(used, rows, 0).astype(jnp.int32)
    return pos, src, owner.astype(jnp.int32), rows


GATHER_ROWS = 256


def _row_copy(src_hbm, src_row, dst_vmem, dst_row, sem):
    return pltpu.make_async_copy(src_hbm.at[pl.ds(src_row, 1)], dst_vmem.at[pl.ds(dst_row, 1)], sem)


def _gather_body(src_ref, tr_ref, h_hbm, o_ref, buf, sem, *, tile):
    i = pl.program_id(0)
    base = i * GATHER_ROWS
    per_tile = tile // GATHER_ROWS
    used = (i % per_tile) * GATHER_ROWS < tr_ref[i // per_tile]

    @pl.when(used)
    def _():
        def issue(r, carry):
            _row_copy(h_hbm, src_ref[base + r], buf, r, sem).start()
            return carry

        def drain(r, carry):
            _row_copy(h_hbm, 0, buf, r, sem).wait()
            return carry

        lax.fori_loop(0, GATHER_ROWS, issue, 0)
        lax.fori_loop(0, GATHER_ROWS, drain, 0)
        o_ref[...] = buf[...].astype(o_ref.dtype)

    @pl.when(jnp.logical_not(used))
    def _():
        o_ref[...] = jnp.zeros(o_ref.shape, o_ref.dtype)


def gather_rows(h, src, tile_rows, tile):
    n = src.shape[0]
    D = h.shape[1]
    assert GATHER_ROWS == MOE_SUB_ROWS and tile % GATHER_ROWS == 0
    return pl.pallas_call(
        functools.partial(_gather_body, tile=tile),
        out_shape=jax.ShapeDtypeStruct((n, D), BF16),
        grid_spec=pltpu.PrefetchScalarGridSpec(
            num_scalar_prefetch=2,
            grid=(n // GATHER_ROWS,),
            in_specs=[pl.BlockSpec(memory_space=pl.ANY)],
            out_specs=pl.BlockSpec((GATHER_ROWS, D), lambda i, src, tr: (i, 0)),
            scratch_shapes=[pltpu.VMEM((GATHER_ROWS, D), F32), pltpu.SemaphoreType.DMA]),
        compiler_params=_params(("arbitrary",)),
        name="gather_rows",
    )(src, tile_rows, h)


def _combine_body(pos_ref, y_hbm, wt_ref, x_ref, g_ref, x2_ref, h_ref, buf, sem, *, bm, n_full, rem):
    i = pl.program_id(0)

    def do(rows):
        base = i * bm

        def issue(r, carry):
            for k in range(TOP_K):
                _row_copy(y_hbm, pos_ref[TOP_K * (base + r) + k], buf.at[k], r, sem).start()
            return carry

        def drain(r, carry):
            for k in range(TOP_K):
                _row_copy(y_hbm, 0, buf.at[k], r, sem).wait()
            return carry

        lax.fori_loop(0, rows, issue, 0)
        lax.fori_loop(0, rows, drain, 0)
        w = wt_ref[:rows]
        y = w[:, 0:1] * buf[0, :rows] + w[:, 1:2] * buf[1, :rows]
        x2 = x_ref[:rows] + y
        x2_ref[:rows] = x2
        h_ref[:rows] = _rms(x2, g_ref[...]).astype(h_ref.dtype)

    _row_branches(i, n_full, bm, rem, do)


def combine_rows(y_sorted, pos, wts, x, g3, layer, bm=256):
    T, D = x.shape
    n_full, rem = divmod(T, bm)
    row = lambda i, pos: (i, 0)
    return pl.pallas_call(
        functools.partial(_combine_body, bm=bm, n_full=n_full, rem=rem),
        out_shape=(jax.ShapeDtypeStruct((T, D), F32), jax.ShapeDtypeStruct((T, D), BF16)),
        grid_spec=pltpu.PrefetchScalarGridSpec(
            num_scalar_prefetch=1,
            grid=(pl.cdiv(T, bm),),
            in_specs=[pl.BlockSpec(memory_space=pl.ANY),
                      pl.BlockSpec((bm, TOP_K), row), pl.BlockSpec((bm, D), row),
                      pl.BlockSpec((None, 1, D), lambda i, pos: (layer, 0, 0))],
            out_specs=(pl.BlockSpec((bm, D), row), pl.BlockSpec((bm, D), row)),
            scratch_shapes=[pltpu.VMEM((TOP_K, bm, D), F32), pltpu.SemaphoreType.DMA]),
        compiler_params=_params(("arbitrary",)),
        name="combine_rows",
    )(pos, y_sorted, wts, x, g3)


def _combine_experts_body(y_ref, idx_ref, wt_ref, x_ref, g_ref, x2_ref, h_ref, *, n_experts):
    rows = x_ref.shape[0]
    idx = idx_ref[...]
    wt = wt_ref[...]
    acc = jnp.zeros(x_ref.shape, F32)
    for e in range(n_experts):
        ce = jnp.sum(jnp.where(idx == e, wt, 0.0), axis=1, keepdims=True)
        acc = acc + ce * y_ref[e * rows:(e + 1) * rows, :]
    x2 = x_ref[...] + acc
    x2_ref[...] = x2
    h_ref[...] = _rms(x2, g_ref[...])


def combine_experts(y, idx, wts, x, g3, layer):
    T, D = x.shape
    E = y.shape[0] // T
    full = lambda a: pl.BlockSpec(a.shape, lambda i: (0,) * a.ndim)
    return pl.pallas_call(
        functools.partial(_combine_experts_body, n_experts=E),
        out_shape=(jax.ShapeDtypeStruct((T, D), F32), jax.ShapeDtypeStruct((T, D), F32)),
        grid=(1,),
        in_specs=[full(y), full(idx), full(wts), full(x), pl.BlockSpec((None, 1, D), lambda i: (layer, 0, 0))],
        out_specs=(full(x), full(x)),
        compiler_params=_params(("arbitrary",)),
        name="combine_experts",
    )(y, idx, wts, x, g3)


def _forward(x_prompt, x_sample, p_prompt, p_sample, cache_k, cache_v, cache_logf, state_pool, state_conv,
             page_table, norm_mix, norm_ffn, norm_ple, w_in, b_forget, q_norm, k_norm, w_pool, pool_scale,
             w_attn_out, conv_w, conv_b, conv_ln_g, conv_ln_b, w_conv_out, w_out, dense_w_gate, dense_w_up,
             dense_w_down, router_w, moe_w_gate, moe_w_up, moe_w_down, w_ple, w_ple_gate):
    B, L, D = x_prompt.shape
    DB, S, _ = x_sample.shape
    depth = norm_mix.shape[0]
    H = b_forget.shape[1]
    A = H * HEAD_DIM
    CP = w_pool.shape[1] * w_pool.shape[2]
    CC = conv_w.shape[2]
    page = cache_k.shape[2]
    past_len = page_table.shape[1] * page
    P, NS = B * L, DB * S
    c_q, c_k, c_v, c_f = CP, CP + A, CP + 2 * A, CP + 3 * A
    c_glu = c_f + H
    assert c_f == 7 * IN_TILE and CP == IN_TILE and A == 2 * IN_TILE and P % ROW_TILE == 0

    vec3 = lambda a: a.reshape(a.shape[0], 1, a.shape[1])
    norm_mix3, norm_ffn3, norm_ple3 = vec3(norm_mix), vec3(norm_ffn), vec3(norm_ple)
    b_forget3, pool_scale3 = vec3(b_forget), vec3(pool_scale)
    conv_b3, ln_g3, ln_b3 = vec3(conv_b), vec3(conv_ln_g), vec3(conv_ln_b)
    w_f = w_in[:, :, c_f:c_glu]
    w_rest32 = w_in[:, :, c_glu:]
    w_rest = w_rest32.astype(BF16)
    cache_kt = cache_k.transpose(0, 1, 3, 4, 2)
    cache_vt = cache_v.transpose(0, 1, 3, 4, 2)
    cache_lft = cache_logf.transpose(0, 1, 3, 2)
    own_head = (jnp.arange(A)[None, :] // HEAD_DIM) == (jnp.arange(S * H)[:, None] % H)
    dense4 = [w.reshape(w.shape[0], 1, *w.shape[1:]) for w in (dense_w_gate, dense_w_up, dense_w_down)]
    zero_pool = jnp.zeros((B, POOL_STATE, CP), F32)
    zero_conv = jnp.zeros((B, CONV_STATE, CC), F32)

    n_dense_tiles = P // ROW_TILE
    dense_te = jnp.zeros((n_dense_tiles,), jnp.int32)
    dense_tr = jnp.full((n_dense_tiles,), ROW_TILE, jnp.int32)
    n_moe_tiles = (TOP_K * P) // ROW_TILE + N_EXPERTS
    one_te = jnp.zeros((1,), jnp.int32)
    one_tr = jnp.full((1,), NS, jnp.int32)
    expert_ids = jnp.arange(N_EXPERTS, dtype=jnp.int32)

    xp = x_prompt.reshape(P, D)
    xs = x_sample.reshape(NS, D)
    outs = [[] for _ in range(10)]
    for i in range(depth):
        j = i // 2

        h = rms_norm_rows(xp, norm_mix3, i)
        z1 = project_pool_qkv(h, w_in, i, q_norm, k_norm)
        lf = project_logf(h, w_f, i, b_forget3)
        u = project_glu(h, w_rest, i, CC)
        gates = project_gates(h, w_rest, i, 2 * CC, 3 * D)
        ya, pool_p = pool_mixer_rows(z1, B, L, 1, zero_pool, w_pool, pool_scale3, i, 0, D)
        yc, conv_p = conv_module_rows(u, B, L, 1, zero_conv, conv_w, conv_b3, ln_g3, ln_b3, w_conv_out, i)
        c = cumsum_rows(lf, B, L).reshape(B, L, H // 2, 2)
        o = prompt_attention(z1, c.transpose(0, 2, 3, 1), B, L, H, c_q, c_k, c_v)
        merged = attn_out_merge(o, w_attn_out, i, gates, ya, yc)
        xp = out_proj_residual(merged, w_out, i, xp)
        if i % 2 == 0:
            h2 = rms_norm_rows(xp, norm_ffn3, i)
            y = ffn_tiles(h2, dense_te, dense_tr, *dense4, j, sub_rows=DENSE_SUB_ROWS)
            xp, h3 = add_norm_rows(xp, y, norm_ple3, i)
        else:
            h2f, idx, wts = route_rows(xp, norm_ffn3, i, router_w, j)
            pos, src, te, tr = moe_dispatch(idx, ROW_TILE, n_moe_tiles)
            x_sorted = gather_rows(h2f, src, tr, ROW_TILE)
            y_sorted = ffn_tiles(x_sorted, te, tr, moe_w_gate, moe_w_up, moe_w_down, j)
            xp, h3 = combine_rows(y_sorted, pos, wts, xp, norm_ple3, i)
        xp = ple_residual(h3, w_ple_gate, p_prompt[i].reshape(P, -1), w_ple, i, xp)

        hs = rms_norm_rows(xs, norm_mix3, i, out_dtype=F32)
        z1s = project_pool_qkv(hs, w_in, i, q_norm, k_norm, precise=True)
        lfs = project_logf(hs, w_f, i, b_forget3, precise=True)
        us = project_glu(hs, w_rest32, i, CC, precise=True)
        gates_s = project_gates(hs, w_rest32, i, 2 * CC, 3 * D, precise=True)
        ya_s, pool_s = pool_mixer_rows(z1s, DB, S, DB, state_pool[i], w_pool, pool_scale3, i, past_len, D,
                                       precise=True)
        yc_s, conv_s = conv_module_rows(us, DB, S, DB, state_conv[i], conv_w, conv_b3, ln_g3, ln_b3,
                                        w_conv_out, i, precise=True)
        q_bd = jnp.where(own_head, jnp.tile(z1s[:, c_q:c_k].reshape(DB, S * H, HEAD_DIM), (1, 1, H)), 0.0)
        new_rows = lambda a: jnp.pad(a.reshape(DB, S, A), ((0, 0), (0, NEW_PAD - S), (0, 0)))
        lfn = jnp.pad(lfs.reshape(DB, S, H).transpose(0, 2, 1), ((0, 0), (0, 0), (0, NEW_PAD - S)))
        o_s = sample_attention(q_bd, new_rows(z1s[:, c_k:c_v]), new_rows(z1s[:, c_v:c_f]), lfn,
                               cache_kt, cache_vt, cache_lft, page_table, i)
        merged_s = attn_out_merge(o_s.reshape(NS, A), w_attn_out, i, gates_s, ya_s, yc_s, precise=True)
        xs = out_proj_residual(merged_s, w_out, i, xs, precise=True)
        if i % 2 == 0:
            h2s = rms_norm_rows(xs, norm_ffn3, i, out_dtype=F32)
            ys = ffn_tiles(h2s, one_te, one_tr, *dense4, j, precise=True)
            xs, h3s = add_norm_rows(xs, ys, norm_ple3, i, out_dtype=F32)
        else:
            h2s, idx_s, wts_s = route_rows(xs, norm_ffn3, i, router_w, j)
            hit = jnp.any(idx_s.reshape(-1)[None, :] == expert_ids[:, None], axis=1)
            ys = ffn_tiles(h2s, expert_ids, jnp.where(hit, NS, 0).astype(jnp.int32), moe_w_gate, moe_w_up,
                           moe_w_down, j, precise=True)
            xs, h3s = combine_experts(ys, idx_s, wts_s, xs, norm_ple3, i)
        xs = ple_residual(h3s, w_ple_gate, p_sample[i].reshape(NS, -1), w_ple, i, xs, precise=True)

        heads = lambda a, n, l: a.reshape(n, l, H, HEAD_DIM)
        for lst, val in zip(outs, (
                heads(z1[:, c_k:c_v], B, L), heads(z1[:, c_v:c_f], B, L), lf.reshape(B, L, H), pool_p, conv_p,
                heads(z1s[:, c_k:c_v], DB, S), heads(z1s[:, c_v:c_f], DB, S), lfs.reshape(DB, S, H), pool_s,
                conv_s)):
            lst.append(val)
    return (xp.reshape(B, L, D), xs.reshape(DB, S, D), *(jnp.stack(l) for l in outs))


def kernel(x_prompt, x_sample, p_prompt, p_sample, cache_k, cache_v, cache_logf, state_pool, state_conv, page_table, norm_mix, norm_ffn, norm_ple, w_in, b_forget, q_norm, k_norm, w_pool, pool_scale, w_attn_out, conv_w, conv_b, conv_ln_g, conv_ln_b, w_conv_out, w_out, dense_w_gate, dense_w_up, dense_w_down, router_w, moe_w_gate, moe_w_up, moe_w_down, w_ple, w_ple_gate):
    return _forward(x_prompt, x_sample, p_prompt, p_sample, cache_k, cache_v, cache_logf, state_pool, state_conv,
                    page_table, norm_mix, norm_ffn, norm_ple, w_in, b_forget, q_norm, k_norm, w_pool, pool_scale,
                    w_attn_out, conv_w, conv_b, conv_ln_g, conv_ln_b, w_conv_out, w_out, dense_w_gate,
                    dense_w_up, dense_w_down, router_w, moe_w_gate, moe_w_up, moe_w_down, w_ple, w_ple_gate)
```

```python
import functools

import jax
import jax.numpy as jnp
from jax import lax
from jax.experimental import pallas as pl
from jax.experimental.pallas import tpu as pltpu

F32 = jnp.float32
BF16 = jnp.bfloat16

HEAD_DIM = 64
POOL_WINDOWS = (2, 4, 8, 16)
POOL_GROUP = 128
POOL_STATE = max(POOL_WINDOWS) - 1
CONV_WIDTH = 31
CONV_STATE = CONV_WIDTH - 1
N_EXPERTS = 8
TOP_K = 2
RMS_EPS = 1e-6
LN_EPS = 1e-5
ATTN_SCALE = HEAD_DIM ** -0.5
LOG2_E = 1.4426950408889634
LANES = 128

V7X_VMEM_LIMIT_BYTES = 56 * 1024 * 1024
ROW_TILE = 1024
MOE_SUB_ROWS = 256
FULL_SUB_ROWS = 512


def _params(sem):
    return pltpu.CompilerParams(dimension_semantics=sem, vmem_limit_bytes=V7X_VMEM_LIMIT_BYTES)


def _row_branches(i, n_full, bm, rem, do):
    if rem == 0:
        do(bm)
        return

    @pl.when(i < n_full)
    def _():
        do(bm)

    @pl.when(i == n_full)
    def _():
        do(rem)


def _rms(x, g):
    return x * lax.rsqrt(jnp.mean(x * x, axis=-1, keepdims=True) + RMS_EPS) * g


_NN = (((1,), (0,)), ((), ()))
_NT = (((1,), (1,)), ((), ()))


def _split(x):
    hi = x.astype(BF16)
    return hi, (x - hi.astype(F32)).astype(BF16)


def _dot3_parts(xh, xl, wh, wl, dims=_NN):
    m = xh.shape[0]
    both = lax.dot_general(jnp.concatenate([xh, xl], axis=0), wh, dims, preferred_element_type=F32)
    return both[:m] + (both[m:] + lax.dot_general(xh, wl, dims, preferred_element_type=F32))


def _dot3(x, w, dims=_NN):
    return _dot3_parts(*_split(x), *_split(w), dims)


def _norm_body(x_ref, g_ref, h_ref, *, bm, n_full, rem):
    def do(rows):
        h_ref[:rows] = _rms(x_ref[:rows], g_ref[...]).astype(h_ref.dtype)

    _row_branches(pl.program_id(0), n_full, bm, rem, do)


def rms_norm_rows(x, g3, layer, out_dtype=BF16, bm=512):
    T, D = x.shape
    bm = min(bm, T)
    n_full, rem = divmod(T, bm)
    return pl.pallas_call(
        functools.partial(_norm_body, bm=bm, n_full=n_full, rem=rem),
        out_shape=jax.ShapeDtypeStruct((T, D), out_dtype),
        grid=(pl.cdiv(T, bm),),
        in_specs=[pl.BlockSpec((bm, D), lambda i: (i, 0)),
                  pl.BlockSpec((None, 1, D), lambda i: (layer, 0, 0))],
        out_specs=pl.BlockSpec((bm, D), lambda i: (i, 0)),
        compiler_params=_params(("arbitrary",)),
        name="rms_norm_rows",
    )(x, g3)


def _add_norm_body(x_ref, y_ref, g_ref, x2_ref, h_ref, *, bm, n_full, rem):
    def do(rows):
        x2 = x_ref[:rows] + y_ref[:rows]
        x2_ref[:rows] = x2
        h_ref[:rows] = _rms(x2, g_ref[...]).astype(h_ref.dtype)

    _row_branches(pl.program_id(0), n_full, bm, rem, do)


def add_norm_rows(x, y, g3, layer, out_dtype=BF16, bm=512):
    T, D = x.shape
    bm = min(bm, T)
    n_full, rem = divmod(T, bm)
    spec = pl.BlockSpec((bm, D), lambda i: (i, 0))
    return pl.pallas_call(
        functools.partial(_add_norm_body, bm=bm, n_full=n_full, rem=rem),
        out_shape=(jax.ShapeDtypeStruct((T, D), F32), jax.ShapeDtypeStruct((T, D), out_dtype)),
        grid=(pl.cdiv(T, bm),),
        in_specs=[spec, spec, pl.BlockSpec((None, 1, D), lambda i: (layer, 0, 0))],
        out_specs=(spec, spec),
        compiler_params=_params(("arbitrary",)),
        name="add_norm_rows",
    )(x, y, g3)


def _mm_body(*refs, n_pairs, n_extra, n_out, cast_w, precise, epilogue, bm, n_full, rem):
    xs = refs[:n_pairs]
    ws = refs[n_pairs:2 * n_pairs]
    extras = refs[2 * n_pairs:2 * n_pairs + n_extra]
    outs = refs[2 * n_pairs + n_extra:2 * n_pairs + n_extra + n_out]
    wbs = refs[2 * n_pairs + n_extra + n_out:]
    j = pl.program_id(0)
    i = pl.program_id(1)

    @pl.when(i == 0)
    def _():
        k = 0
        for p in range(n_pairs):
            if cast_w[p]:
                wbs[k][...] = ws[p][...].astype(BF16)
                k += 1

    def do(rows):
        accs = []
        k = 0
        for p in range(n_pairs):
            x = xs[p][:rows]
            if precise:
                accs.append(_dot3(x, ws[p][...]))
                continue
            if x.dtype != BF16:
                x = x.astype(BF16)
            if cast_w[p]:
                w = wbs[k][...]
                k += 1
            else:
                w = ws[p][...]
            accs.append(jnp.dot(x, w, preferred_element_type=F32))
        epilogue(accs, rows, j, extras, outs)

    _row_branches(i, n_full, bm, rem, do)


def matmul_rows(pairs, extras, outs, epilogue, *, tn, n_tiles, name, bm, precise=False):
    T = pairs[0][0].shape[0]
    n_full, rem = divmod(T, bm)
    in_specs, args, cast_w, scratch = [], [], [], []
    for x, _, _, _ in pairs:
        in_specs.append(pl.BlockSpec((bm, x.shape[1]), lambda j, i: (i, 0)))
        args.append(x)
    for x, w, lead, off in pairs:
        K = x.shape[1]
        blk = (None,) * len(lead) + (K, tn)
        in_specs.append(pl.BlockSpec(blk, lambda j, i, lead=lead, off=off: (*lead, 0, j + off)))
        args.append(w)
        assert not precise or (w.dtype == F32 and x.dtype == F32)
        cast_w.append(w.dtype != BF16 and not precise)
        if cast_w[-1]:
            scratch.append(pltpu.VMEM((K, tn), BF16))
    for a, blk, imap in extras:
        in_specs.append(pl.BlockSpec(blk, imap))
        args.append(a)
    return pl.pallas_call(
        functools.partial(_mm_body, n_pairs=len(pairs), n_extra=len(extras), n_out=len(outs),
                          cast_w=tuple(cast_w), precise=precise, epilogue=epilogue, bm=bm, n_full=n_full,
                          rem=rem),
        out_shape=tuple(o[0] for o in outs),
        grid=(n_tiles, pl.cdiv(T, bm)),
        in_specs=in_specs,
        out_specs=tuple(pl.BlockSpec(o[1], o[2]) for o in outs),
        scratch_shapes=scratch,
        compiler_params=_params(("arbitrary", "arbitrary")),
        name=name,
    )(*args)


def _tile_spec(bm, tn, off=0):
    return (bm, tn), (lambda j, i, off=off: (i, j + off))


IN_TILE = 512


def _epi_pool_qkv(accs, rows, j, extras, outs):
    (acc,) = accs
    bd_ref, g_ref = extras
    (o_ref,) = outs
    is_head_norm = (j >= 1) & (j <= 4)

    @pl.when(is_head_norm)
    def _():
        hi, lo = _split(acc * acc)
        ss = (jnp.dot(hi, bd_ref[...], preferred_element_type=F32)
              + jnp.dot(lo, bd_ref[...], preferred_element_type=F32))
        o_ref[:rows] = acc * lax.rsqrt(ss * (1.0 / HEAD_DIM) + RMS_EPS) * g_ref[...]

    @pl.when(jnp.logical_not(is_head_norm))
    def _():
        o_ref[:rows] = acc


def project_pool_qkv(h, w_in, layer, q_norm, k_norm, precise=False):
    T = h.shape[0]
    tn = IN_TILE
    bm = min(ROW_TILE, T)
    n_tiles = 7
    heads_per_tile = tn // HEAD_DIM
    ones = jnp.ones((tn,), F32)
    gq = jnp.tile(q_norm[layer], heads_per_tile)
    gk = jnp.tile(k_norm[layer], heads_per_tile)
    gains = jnp.stack([ones, gq, gq, gk, gk, ones, ones]).reshape(n_tiles, 1, tn)
    grp = jnp.arange(tn) // HEAD_DIM
    bd = (grp[:, None] == grp[None, :]).astype(BF16)
    blk, imap = _tile_spec(bm, tn)
    (z1,) = matmul_rows(
        [(h, w_in, (layer,), 0)],
        [(bd, (tn, tn), lambda j, i: (0, 0)), (gains, (None, 1, tn), lambda j, i: (j, 0, 0))],
        [(jax.ShapeDtypeStruct((T, n_tiles * tn), F32), blk, imap)],
        _epi_pool_qkv, tn=tn, n_tiles=n_tiles, name="project_pool_qkv", bm=bm, precise=precise)
    return z1


def _epi_logf(accs, rows, j, extras, outs):
    (acc,) = accs
    (b_ref,) = extras
    (o_ref,) = outs
    o_ref[:rows] = jax.nn.log_sigmoid(acc + b_ref[...])


def project_logf(h, w_f, layer, b_forget3, precise=False):
    T = h.shape[0]
    H = w_f.shape[-1]
    bm = min(ROW_TILE, T)
    (lf,) = matmul_rows(
        [(h, w_f, (layer,), 0)],
        [(b_forget3, (None, 1, H), lambda j, i: (layer, 0, 0))],
        [(jax.ShapeDtypeStruct((T, H), F32), (bm, H), lambda j, i: (i, 0))],
        _epi_logf, tn=H, n_tiles=1, name="project_logf", bm=bm, precise=precise)
    return lf


def _epi_glu(accs, rows, j, extras, outs):
    a, b = accs
    (o_ref,) = outs
    o_ref[:rows] = a * jax.nn.sigmoid(b)


def project_glu(h, w_rest, layer, conv_dim, precise=False):
    T = h.shape[0]
    bm = min(ROW_TILE, T)
    (u,) = matmul_rows(
        [(h, w_rest, (layer,), 0), (h, w_rest, (layer,), 1)],
        [],
        [(jax.ShapeDtypeStruct((T, conv_dim), F32), (bm, conv_dim), lambda j, i: (i, 0))],
        _epi_glu, tn=conv_dim, n_tiles=1, name="project_glu", bm=bm, precise=precise)
    return u


def _epi_sigmoid(accs, rows, j, extras, outs):
    (acc,) = accs
    (o_ref,) = outs
    o_ref[:rows] = jax.nn.sigmoid(acc)


def project_gates(h, w_rest, layer, col0, n_cols, tn=1024, precise=False):
    T = h.shape[0]
    bm = min(ROW_TILE, T)
    blk, imap = _tile_spec(bm, tn)
    (g,) = matmul_rows(
        [(h, w_rest, (layer,), col0 // tn)],
        [],
        [(jax.ShapeDtypeStruct((T, n_cols), F32), blk, imap)],
        _epi_sigmoid, tn=tn, n_tiles=n_cols // tn, name="project_gates", bm=bm, precise=precise)
    return g


def _epi_merge(accs, rows, j, extras, outs):
    (yb,) = accs
    g0, g1, g2, ya, yc = extras
    (o_ref,) = outs
    merged = g0[:rows] * ya[:rows] + g1[:rows] * yb + g2[:rows] * yc[:rows]
    o_ref[:rows] = merged.astype(o_ref.dtype)


def attn_out_merge(o, w_attn_out, layer, gates, ya, yc, tn=1024, precise=False):
    T = o.shape[0]
    D = w_attn_out.shape[-1]
    nt = D // tn
    bm = min(ROW_TILE // 2, T)
    blk, imap = _tile_spec(bm, tn)
    (m,) = matmul_rows(
        [(o, w_attn_out, (layer,), 0)],
        [(gates, *_tile_spec(bm, tn, 0)), (gates, *_tile_spec(bm, tn, nt)),
         (gates, *_tile_spec(bm, tn, 2 * nt)), (ya, blk, imap), (yc, blk, imap)],
        [(jax.ShapeDtypeStruct((T, D), F32 if precise else BF16), blk, imap)],
        _epi_merge, tn=tn, n_tiles=nt, name="attn_out_merge", bm=bm, precise=precise)
    return m


def _epi_residual(accs, rows, j, extras, outs):
    (acc,) = accs
    (x_ref,) = extras
    (o_ref,) = outs
    o_ref[:rows] = x_ref[:rows] + acc


def out_proj_residual(merged, w_out, layer, x, tn=1024, precise=False):
    T, D = x.shape
    bm = min(ROW_TILE, T)
    blk, imap = _tile_spec(bm, tn)
    (x2,) = matmul_rows(
        [(merged, w_out, (layer,), 0)],
        [(x, blk, imap)],
        [(jax.ShapeDtypeStruct((T, D), F32), blk, imap)],
        _epi_residual, tn=tn, n_tiles=D // tn, name="out_proj_residual", bm=bm, precise=precise)
    return x2


def _epi_ple(accs, rows, j, extras, outs):
    gate_logit, emb = accs
    (x_ref,) = extras
    (o_ref,) = outs
    o_ref[:rows] = x_ref[:rows] + jax.nn.sigmoid(gate_logit) * emb


def ple_residual(h, w_ple_gate, p, w_ple, layer, x, tn=1024, precise=False):
    T, D = x.shape
    bm = min(ROW_TILE // 2, T)
    blk, imap = _tile_spec(bm, tn)
    (x2,) = matmul_rows(
        [(h, w_ple_gate, (layer,), 0), (p, w_ple, (layer,), 0)],
        [(x, blk, imap)],
        [(jax.ShapeDtypeStruct((T, D), F32), blk, imap)],
        _epi_ple, tn=tn, n_tiles=D // tn, name="ple_residual", bm=bm, precise=precise)
    return x2


POOL_PAD = 16
CONV_PAD = 32


def _pool_body(u_ref, pre_ref, w_ref, sc_ref, ya_ref, st_ref, ext_ref, *, nb, L, R, offset, precise):
    C = ext_ref.shape[-1]
    c = pl.program_id(1)

    @pl.when(c == 0)
    def _():
        for lb in range(nb):
            ext_ref[lb, 0:POOL_PAD - POOL_STATE, :] = jnp.zeros((POOL_PAD - POOL_STATE, C), F32)
            ext_ref[lb, POOL_PAD - POOL_STATE:POOL_PAD, :] = pre_ref[lb]
            ext_ref[lb, POOL_PAD:POOL_PAD + L, :] = u_ref[lb * L:(lb + 1) * L, :]
            st_ref[lb] = ext_ref[lb, POOL_PAD + L - POOL_STATE:POOL_PAD + L, :]

    t0 = 0 if L == R else pl.multiple_of(c * R, R)
    pos = offset + t0 + lax.broadcasted_iota(jnp.int32, (R, 1), 0)
    wins = [ext_ref[lb, pl.ds(t0, R + POOL_PAD), :] for lb in range(nb)]
    for g, w in enumerate(POOL_WINDOWS):
        cnt = jnp.minimum(pos + 1, w).astype(F32)
        rs = []
        for lb in range(nb):
            wg = wins[lb][:, g * POOL_GROUP:(g + 1) * POOL_GROUP]
            cur = wg[POOL_PAD:POOL_PAD + R]
            acc = cur
            for i in range(1, w):
                acc = acc + wg[POOL_PAD - i:POOL_PAD - i + R]
            rs.append(acc / cnt - cur)
        r = rs[0] if nb == 1 else jnp.concatenate(rs, axis=0)
        if precise:
            y = _dot3(r, w_ref[g])
        else:
            y = jnp.dot(r.astype(BF16), w_ref[g].astype(BF16), preferred_element_type=F32)
        og = y.shape[1]
        ya_ref[:, g * og:(g + 1) * og] = y * sc_ref[:, g * og:(g + 1) * og]


def pool_mixer_rows(z1, nseq, L, nb, prefix, w_pool, pool_scale3, layer, offset, d_model, precise=False):
    C = len(POOL_WINDOWS) * POOL_GROUP
    R = min(L, 512)
    n_chunks = L // R
    return pl.pallas_call(
        functools.partial(_pool_body, nb=nb, L=L, R=R, offset=offset, precise=precise),
        out_shape=(jax.ShapeDtypeStruct((nseq * L, d_model), F32),
                   jax.ShapeDtypeStruct((nseq, POOL_STATE, C), F32)),
        grid=(nseq // nb, n_chunks),
        in_specs=[pl.BlockSpec((nb * L, C), lambda s, c: (s, 0)),
                  pl.BlockSpec((nb, POOL_STATE, C), lambda s, c: (s, 0, 0)),
                  pl.BlockSpec((None,) + w_pool.shape[1:], lambda s, c: (layer, 0, 0, 0)),
                  pl.BlockSpec((None, 1, d_model), lambda s, c: (layer, 0, 0))],
        out_specs=(pl.BlockSpec((nb * R, d_model), lambda s, c: (s * n_chunks + c, 0)),
                   pl.BlockSpec((nb, POOL_STATE, C), lambda s, c: (s, 0, 0))),
        scratch_shapes=[pltpu.VMEM((nb, POOL_PAD + L, C), F32)],
        compiler_params=_params(("arbitrary", "arbitrary")),
        name="pool_mixer_rows",
    )(z1, prefix, w_pool, pool_scale3)


def _conv_body(u_ref, pre_ref, cw_ref, cb_ref, lg_ref, lb_ref, wo_ref, yc_ref, st_ref, ext_ref, *, nb, L, R,
               precise):
    C = ext_ref.shape[-1]
    c = pl.program_id(1)

    @pl.when(c == 0)
    def _():
        for lb in range(nb):
            ext_ref[lb, 0:CONV_PAD - CONV_STATE, :] = jnp.zeros((CONV_PAD - CONV_STATE, C), F32)
            ext_ref[lb, CONV_PAD - CONV_STATE:CONV_PAD, :] = pre_ref[lb]
            ext_ref[lb, CONV_PAD:CONV_PAD + L, :] = u_ref[lb * L:(lb + 1) * L, :]
            st_ref[lb] = ext_ref[lb, CONV_PAD + L - CONV_STATE:CONV_PAD + L, :]

    t0 = 0 if L == R else pl.multiple_of(c * R, R)
    ys = []
    for lb in range(nb):
        win = ext_ref[lb, pl.ds(t0, R + CONV_PAD), :]
        acc = jnp.zeros((R, C), F32)
        for j in range(CONV_WIDTH):
            s0 = CONV_PAD - CONV_STATE + j
            acc = acc + win[s0:s0 + R] * cw_ref[j:j + 1, :]
        ys.append(acc)
    y = (ys[0] if nb == 1 else jnp.concatenate(ys, axis=0)) + cb_ref[...]
    mu = jnp.mean(y, axis=-1, keepdims=True)
    yc = y - mu
    yn = yc * lax.rsqrt(jnp.mean(yc * yc, axis=-1, keepdims=True) + LN_EPS) * lg_ref[...] + lb_ref[...]
    act = yn * jax.nn.sigmoid(yn)
    if precise:
        yc_ref[...] = _dot3(act, wo_ref[...])
    else:
        yc_ref[...] = jnp.dot(act.astype(BF16), wo_ref[...].astype(BF16), preferred_element_type=F32)


def conv_module_rows(u, nseq, L, nb, prefix, conv_w, conv_b3, ln_g3, ln_b3, w_conv_out, layer, precise=False):
    C = u.shape[1]
    D = w_conv_out.shape[-1]
    R = min(L, 256)
    n_chunks = L // R
    vec = lambda a: pl.BlockSpec((None, 1, a.shape[-1]), lambda s, c: (layer, 0, 0))
    return pl.pallas_call(
        functools.partial(_conv_body, nb=nb, L=L, R=R, precise=precise),
        out_shape=(jax.ShapeDtypeStruct((nseq * L, D), F32), jax.ShapeDtypeStruct((nseq, CONV_STATE, C), F32)),
        grid=(nseq // nb, n_chunks),
        in_specs=[pl.BlockSpec((nb * L, C), lambda s, c: (s, 0)),
                  pl.BlockSpec((nb, CONV_STATE, C), lambda s, c: (s, 0, 0)),
                  pl.BlockSpec((None, CONV_WIDTH, C), lambda s, c: (layer, 0, 0)),
                  vec(conv_b3), vec(ln_g3), vec(ln_b3),
                  pl.BlockSpec((None, C, D), lambda s, c: (layer, 0, 0))],
        out_specs=(pl.BlockSpec((nb * R, D), lambda s, c: (s * n_chunks + c, 0)),
                   pl.BlockSpec((nb, CONV_STATE, C), lambda s, c: (s, 0, 0))),
        scratch_shapes=[pltpu.VMEM((nb, CONV_PAD + L, C), F32)],
        compiler_params=_params(("arbitrary", "arbitrary")),
        name="conv_module_rows",
    )(u, prefix, conv_w, conv_b3, ln_g3, ln_b3, w_conv_out)


CUMSUM_CHUNK = 128


def _cumsum_body(x_ref, o_ref, *, L):
    n = CUMSUM_CHUNK
    tri = (lax.broadcasted_iota(jnp.int32, (n, n), 0) >= lax.broadcasted_iota(jnp.int32, (n, n), 1)).astype(F32)
    carry = jnp.zeros((1, x_ref.shape[-1]), F32)
    for k in range(L // n):
        ck = jnp.dot(tri, x_ref[k * n:(k + 1) * n, :], precision=lax.Precision.HIGHEST,
                     preferred_element_type=F32) + carry
        o_ref[k * n:(k + 1) * n, :] = ck
        carry = ck[n - 1:n, :]


def cumsum_rows(logf, nseq, L):
    H = logf.shape[1]
    return pl.pallas_call(
        functools.partial(_cumsum_body, L=L),
        out_shape=jax.ShapeDtypeStruct((nseq * L, H), F32),
        grid=(nseq,),
        in_specs=[pl.BlockSpec((L, H), lambda b: (b, 0))],
        out_specs=pl.BlockSpec((L, H), lambda b: (b, 0)),
        compiler_params=_params(("arbitrary",)),
        name="cumsum_rows",
    )(logf)


def _flash_body(q_ref, k_ref, v_ref, ck_ref, o_ref, m_ref, l_ref, acc_ref, *, blk):
    qi = pl.program_id(2)
    q2 = q_ref[...] * (ATTN_SCALE * LOG2_E)
    first = lax.broadcasted_iota(jnp.int32, (1, 2 * HEAD_DIM), 1) < HEAD_DIM
    qh = (jnp.where(first, q2, 0.0).astype(BF16), jnp.where(first, 0.0, q2).astype(BF16))
    m_ref[...] = jnp.full(m_ref.shape, -jnp.inf, F32)
    l_ref[...] = jnp.zeros(l_ref.shape, F32)
    acc_ref[...] = jnp.zeros(acc_ref.shape, F32)
    causal = (lax.broadcasted_iota(jnp.int32, (blk, blk), 0) >= lax.broadcasted_iota(jnp.int32, (blk, blk), 1))

    def update(chunks):
        ks, vs, cks = [], [], []
        for kc, _ in chunks:
            k0 = pl.multiple_of(kc * blk, blk)
            ks.append(k_ref[pl.ds(k0, blk), :].astype(BF16))
            vs.append(v_ref[pl.ds(k0, blk), :].astype(BF16))
            cks.append(ck_ref[:, pl.ds(k0, blk)] * LOG2_E)
        for h in range(2):
            ss = []
            for (_, masked), k2, ck in zip(chunks, ks, cks):
                s = lax.dot_general(qh[h], k2, _NT, preferred_element_type=F32) - ck[h:h + 1, :]
                ss.append(jnp.where(causal, s, -jnp.inf) if masked else s)
            m_prev = m_ref[h]
            m_new = m_prev
            for s in ss:
                m_new = jnp.maximum(m_new, jnp.max(s, axis=1, keepdims=True))
            alpha = jnp.exp2(m_prev - m_new)
            l_new = alpha * l_ref[h]
            acc = alpha * acc_ref[h]
            for s, v2 in zip(ss, vs):
                p = jnp.exp2(s - m_new)
                l_new = l_new + jnp.sum(p, axis=1, keepdims=True)
                acc = acc + jnp.dot(p.astype(BF16), v2, preferred_element_type=F32)
            l_ref[h] = l_new
            acc_ref[h] = acc
            m_ref[h] = m_new

    def body(pair, carry):
        update([(2 * pair, False), (2 * pair + 1, False)])
        return carry

    lax.fori_loop(0, qi // 2, body, 0)

    @pl.when(qi % 2 == 1)
    def _():
        update([(qi - 1, False), (qi, True)])

    @pl.when(qi % 2 == 0)
    def _():
        update([(qi, True)])

    out = jnp.where(first, acc_ref[0] / l_ref[0], acc_ref[1] / l_ref[1])
    o_ref[...] = out.astype(o_ref.dtype)


def prompt_attention(z1, ct_pairs, nseq, L, n_heads, q_col, k_col, v_col):
    blk = min(L, 512)
    nq = L // blk
    lanes = 2 * HEAD_DIM
    qc, kc, vc = q_col // lanes, k_col // lanes, v_col // lanes
    return pl.pallas_call(
        functools.partial(_flash_body, blk=blk),
        out_shape=jax.ShapeDtypeStruct((nseq * L, n_heads * HEAD_DIM), BF16),
        grid=(nseq, n_heads // 2, nq),
        in_specs=[pl.BlockSpec((blk, lanes), lambda b, hp, qi: (b * nq + qi, qc + hp)),
                  pl.BlockSpec((L, lanes), lambda b, hp, qi: (b, kc + hp)),
                  pl.BlockSpec((L, lanes), lambda b, hp, qi: (b, vc + hp)),
                  pl.BlockSpec((None, None, 2, L), lambda b, hp, qi: (b, hp, 0, 0))],
        out_specs=pl.BlockSpec((blk, lanes), lambda b, hp, qi: (b * nq + qi, hp)),
        scratch_shapes=[pltpu.VMEM((2, blk, 1), F32), pltpu.VMEM((2, blk, 1), F32),
                        pltpu.VMEM((2, blk, lanes), F32)],
        compiler_params=_params(("arbitrary", "arbitrary", "arbitrary")),
        name="prompt_attention",
    )(z1, z1, z1, ct_pairs)


DECODE_PAGES_PER_STEP = 8
NEW_PAD = 16


def _decode_body(pt_ref, q_ref, kn_ref, vn_ref, lfn_ref, *rest, PG, H, page):
    k_refs, v_refs, lf_refs = rest[:PG], rest[PG:2 * PG], rest[2 * PG:3 * PG]
    o_ref, m_ref, l_ref, acc_ref, carry_ref, cq_ref = rest[3 * PG:]
    s = pl.program_id(1)
    TH, HD = q_ref.shape
    S = TH // H
    sh_h = H.bit_length() - 1
    qh, ql = _split(q_ref[...] * ATTN_SCALE)
    row = lax.broadcasted_iota(jnp.int32, (TH, 1), 0)
    rows_of = lambda a: jnp.concatenate([a] * S, axis=0)

    @pl.when(s == 0)
    def _():
        upper = (lax.broadcasted_iota(jnp.int32, (NEW_PAD, NEW_PAD), 0)
                 <= lax.broadcasted_iota(jnp.int32, (NEW_PAD, NEW_PAD), 1)).astype(F32)
        cn = rows_of(jnp.dot(lfn_ref[...], upper, precision=lax.Precision.HIGHEST,
                             preferred_element_type=F32))
        jj = lax.broadcasted_iota(jnp.int32, (TH, NEW_PAD), 1)
        tt = row >> sh_h
        cq = jnp.sum(jnp.where(jj == tt, cn, 0.0), axis=1, keepdims=True)
        cq_ref[...] = cq
        sc = _dot3_parts(qh, ql, *_split(kn_ref[...]), _NT) + (cq - cn)
        sc = jnp.where(jj <= tt, sc, -jnp.inf)
        m = jnp.max(sc, axis=1, keepdims=True)
        p = jnp.exp(sc - m)
        m_ref[...] = m
        l_ref[...] = jnp.sum(p, axis=1, keepdims=True)
        acc_ref[...] = _dot3(p, vn_ref[...])
        carry_ref[...] = jnp.zeros(carry_ref.shape, F32)

    lane = lax.broadcasted_iota(jnp.int32, (1, page), 1)
    carry = carry_ref[...]
    biases = []
    for r in range(PG):
        lf = lf_refs[r][...]
        after = jnp.where(lane < page - 1, pltpu.roll(lf, page - 1, axis=1), 0.0)
        sh = 1
        while sh < page:
            after = after + jnp.where(lane < page - sh, pltpu.roll(after, page - sh, axis=1), 0.0)
            sh *= 2
        biases.append(rows_of(after) + carry)
        carry = carry + rows_of(jnp.sum(lf, axis=1, keepdims=True))
    carry_ref[...] = carry

    kT = jnp.concatenate([k_refs[r][...].reshape(HD, page) for r in range(PG)], axis=1)
    sc = _dot3_parts(qh, ql, *_split(kT)) + (jnp.concatenate(biases, axis=1) + cq_ref[...])
    m_prev = m_ref[...]
    m_new = jnp.maximum(m_prev, jnp.max(sc, axis=1, keepdims=True))
    alpha = jnp.exp(m_prev - m_new)
    p = jnp.exp(sc - m_new)
    l_new = alpha * l_ref[...] + jnp.sum(p, axis=1, keepdims=True)
    vT = jnp.concatenate([v_refs[r][...].reshape(HD, page) for r in range(PG)], axis=1)
    acc = alpha * acc_ref[...] + _dot3(p, vT, _NT)
    m_ref[...] = m_new
    l_ref[...] = l_new
    acc_ref[...] = acc

    @pl.when(s == pl.num_programs(1) - 1)
    def _():
        own_head = (lax.broadcasted_iota(jnp.int32, (TH, HD), 1) >> (HEAD_DIM.bit_length() - 1)) == (row & (H - 1))
        od = jnp.where(own_head, acc / l_new, 0.0)
        o_ref[...] = jnp.concatenate([jnp.sum(od[t * H:(t + 1) * H], axis=0, keepdims=True) for t in range(S)],
                                     axis=0)


def sample_attention(q_bd, kn, vn, lfn, cache_kt, cache_vt, cache_lft, page_table, layer):
    B, TH, HD = q_bd.shape
    _, _, H, Dh, page = cache_kt.shape
    S = TH // H
    n_pages = page_table.shape[1]
    PG = DECODE_PAGES_PER_STEP
    assert n_pages % PG == 0 and H & (H - 1) == 0 and Dh == HEAD_DIM and page % LANES == 0 and S <= NEW_PAD

    def page_map(r):
        return lambda b, s, pt: (layer, pt[b, n_pages - 1 - (s * PG + r)], 0, 0, 0)

    def lf_map(r):
        return lambda b, s, pt: (layer, pt[b, n_pages - 1 - (s * PG + r)], 0, 0)

    per_batch = lambda a: pl.BlockSpec((None,) + a.shape[1:], lambda b, s, pt: (b, 0, 0))
    in_specs = [per_batch(q_bd), per_batch(kn), per_batch(vn), per_batch(lfn)]
    in_specs += [pl.BlockSpec((None, None, H, Dh, page), page_map(r)) for r in range(PG)]
    in_specs += [pl.BlockSpec((None, None, H, Dh, page), page_map(r)) for r in range(PG)]
    in_specs += [pl.BlockSpec((None, None, H, page), lf_map(r)) for r in range(PG)]
    return pl.pallas_call(
        functools.partial(_decode_body, PG=PG, H=H, page=page),
        out_shape=jax.ShapeDtypeStruct((B, S, HD), F32),
        grid_spec=pltpu.PrefetchScalarGridSpec(
            num_scalar_prefetch=1,
            grid=(B, n_pages // PG),
            in_specs=in_specs,
            out_specs=pl.BlockSpec((None, S, HD), lambda b, s, pt: (b, 0, 0)),
            scratch_shapes=[pltpu.VMEM((TH, 1), F32), pltpu.VMEM((TH, 1), F32), pltpu.VMEM((TH, HD), F32),
                            pltpu.VMEM((TH, 1), F32), pltpu.VMEM((TH, 1), F32)]),
        compiler_params=_params(("arbitrary", "arbitrary")),
        name="sample_attention",
    )(page_table, q_bd, kn, vn, lfn, *([cache_kt] * PG), *([cache_vt] * PG), *([cache_lft] * PG))


FFN_CHUNK = 512


def _ffn_body(te_ref, tr_ref, x_ref, wg_ref, wu_ref, wd_ref, o_ref, *scratch, bm, sub, full, precise):
    s = pl.program_id(0)
    f = pl.program_id(1)
    rows_valid = tr_ref[s]

    def accumulate(r0, n, y):
        @pl.when(f == 0)
        def _():
            o_ref[r0:r0 + n] = y

        @pl.when(f > 0)
        def _():
            o_ref[r0:r0 + n] += y

    if precise:
        @pl.when(rows_valid > 0)
        def _():
            x = x_ref[...]
            g = _dot3(x, wg_ref[...])
            u = _dot3(x, wu_ref[...])
            accumulate(0, bm, _dot3(g * jax.nn.sigmoid(g) * u, wd_ref[...]))
    else:
        wgb, wub, wdb = scratch

        def block(r0, n):
            x = x_ref[r0:r0 + n]
            g = jnp.dot(x, wgb[...], preferred_element_type=F32)
            u = jnp.dot(x, wub[...], preferred_element_type=F32)
            a = (g * jax.nn.sigmoid(g) * u).astype(BF16)
            accumulate(r0, n, jnp.dot(a, wdb[...], preferred_element_type=F32))

        def cast_weights():
            wgb[...] = wg_ref[...].astype(BF16)
            wub[...] = wu_ref[...].astype(BF16)
            wdb[...] = wd_ref[...].astype(BF16)

        @pl.when(rows_valid == bm)
        def _():
            cast_weights()
            for r0 in range(0, bm, full):
                block(r0, full)

        @pl.when((rows_valid > 0) & (rows_valid < bm))
        def _():
            cast_weights()
            block(0, sub)
            for sb in range(1, bm // sub):
                @pl.when(sb * sub < rows_valid)
                def _():
                    block(sb * sub, sub)

                @pl.when((sb * sub >= rows_valid) & (f == 0))
                def _():
                    o_ref[sb * sub:(sb + 1) * sub] = jnp.zeros((sub, o_ref.shape[1]), F32)

    @pl.when((rows_valid == 0) & (f == 0))
    def _():
        o_ref[...] = jnp.zeros(o_ref.shape, F32)


def ffn_tiles(x, tile_expert, tile_rows, w_gate, w_up, w_down, lw, bm=ROW_TILE, precise=False):
    D = x.shape[1]
    F = w_gate.shape[-1]
    tf = FFN_CHUNK
    nf = F // tf
    bm = min(bm, x.shape[0])
    n_tiles = tile_expert.shape[0]
    assert precise or x.shape[0] == n_tiles * bm

    def f_eff(s, f, tr):
        return jnp.where(tr[s] > 0, f, nf - 1)

    x_spec = (pl.BlockSpec((bm, D), lambda s, f, te, tr: (0, 0)) if precise else
              pl.BlockSpec((bm, D), lambda s, f, te, tr: (s, 0), pipeline_mode=pl.Buffered(1)))
    in_specs = [x_spec,
                pl.BlockSpec((None, None, D, tf), lambda s, f, te, tr: (lw, te[s], 0, f_eff(s, f, tr))),
                pl.BlockSpec((None, None, D, tf), lambda s, f, te, tr: (lw, te[s], 0, f_eff(s, f, tr))),
                pl.BlockSpec((None, None, tf, D), lambda s, f, te, tr: (lw, te[s], f_eff(s, f, tr), 0))]
    scratch = [] if precise else [pltpu.VMEM((D, tf), BF16), pltpu.VMEM((D, tf), BF16),
                                  pltpu.VMEM((tf, D), BF16)]
    return pl.pallas_call(
        functools.partial(_ffn_body, bm=bm, sub=min(MOE_SUB_ROWS, bm), full=min(FULL_SUB_ROWS, bm),
                          precise=precise),
        out_shape=jax.ShapeDtypeStruct((n_tiles * bm, D), F32),
        grid_spec=pltpu.PrefetchScalarGridSpec(
            num_scalar_prefetch=2,
            grid=(n_tiles, nf),
            in_specs=in_specs,
            out_specs=pl.BlockSpec((bm, D), lambda s, f, te, tr: (s, 0), pipeline_mode=pl.Buffered(1)),
            scratch_shapes=scratch),
        compiler_params=_params(("arbitrary", "arbitrary")),
        name="ffn_tiles",
    )(tile_expert, tile_rows, x, w_gate, w_up, w_down)


def _router_body(x_ref, g_ref, wr_ref, h_ref, idx_ref, wt_ref, *, bm, n_full, rem):
    def do(rows):
        h = _rms(x_ref[:rows], g_ref[...])
        h_ref[:rows] = h
        logits = jnp.dot(h, wr_ref[...], precision=lax.Precision.HIGHEST, preferred_element_type=F32)
        E = logits.shape[1]
        lane = lax.broadcasted_iota(jnp.int32, (rows, E), 1)
        m1 = jnp.max(logits, axis=1, keepdims=True)
        i1 = jnp.min(jnp.where(logits == m1, lane, E), axis=1, keepdims=True)
        rest = jnp.where(lane == i1, -jnp.inf, logits)
        m2 = jnp.max(rest, axis=1, keepdims=True)
        i2 = jnp.min(jnp.where(rest == m2, lane, E), axis=1, keepdims=True)
        e = jnp.exp(m2 - m1)
        w1 = 1.0 / (1.0 + e)
        slot = lax.broadcasted_iota(jnp.int32, (rows, TOP_K), 1)
        idx_ref[:rows] = jnp.where(slot == 0, i1, i2)
        wt_ref[:rows] = jnp.where(slot == 0, w1, e * w1)

    _row_branches(pl.program_id(0), n_full, bm, rem, do)


def route_rows(x, g3, layer, router_w, lw, bm=512):
    T, D = x.shape
    E = router_w.shape[-1]
    bm = min(bm, T)
    n_full, rem = divmod(T, bm)
    return pl.pallas_call(
        functools.partial(_router_body, bm=bm, n_full=n_full, rem=rem),
        out_shape=(jax.ShapeDtypeStruct((T, D), F32), jax.ShapeDtypeStruct((T, TOP_K), jnp.int32),
                   jax.ShapeDtypeStruct((T, TOP_K), F32)),
        grid=(pl.cdiv(T, bm),),
        in_specs=[pl.BlockSpec((bm, D), lambda i: (i, 0)),
                  pl.BlockSpec((None, 1, D), lambda i: (layer, 0, 0)),
                  pl.BlockSpec((None, D, E), lambda i: (lw, 0, 0))],
        out_specs=(pl.BlockSpec((bm, D), lambda i: (i, 0)), pl.BlockSpec((bm, TOP_K), lambda i: (i, 0)),
                   pl.BlockSpec((bm, TOP_K), lambda i: (i, 0))),
        compiler_params=_params(("arbitrary",)),
        name="route_rows",
    )(x, g3, router_w)


def moe_dispatch(idx, bm, n_tiles):
    T = idx.shape[0]
    e_flat = idx.reshape(-1)
    onehot = (e_flat[:, None] == jnp.arange(N_EXPERTS, dtype=jnp.int32)[None, :]).astype(jnp.int32)
    incl = jnp.cumsum(onehot, axis=0)
    counts = incl[-1]
    rank = jnp.sum((incl - onehot) * onehot, axis=1)
    tiles_e = (counts + bm - 1) // bm
    tile_end = jnp.cumsum(tiles_e)
    tile_start = tile_end - tiles_e
    pos = (tile_start[e_flat] * bm + rank).astype(jnp.int32)
    src = jnp.zeros((n_tiles * bm,), jnp.int32).at[pos].set(jnp.arange(TOP_K * T, dtype=jnp.int32) // TOP_K)
    tile_ids = jnp.arange(n_tiles, dtype=jnp.int32)
    used = tile_ids < tile_end[-1]
    owner = jnp.sum((tile_ids[:, None] >= tile_end[None, :]).astype(jnp.int32), axis=1)
    owner = jnp.where(used, owner, owner[jnp.maximum(tile_end[-1] - 1, 0)])
    rows = jnp.clip(counts[owner] - (tile_ids - tile_start[owner]) * bm, 0, bm)
    rows = jnp.where(used, rows, 0).astype(jnp.int32)
    return pos, src, owner.astype(jnp.int32), rows


GATHER_ROWS = 256


def _row_copy(src_hbm, src_row, dst_vmem, dst_row, sem):
    return pltpu.make_async_copy(src_hbm.at[pl.ds(src_row, 1)], dst_vmem.at[pl.ds(dst_row, 1)], sem)


def _gather_body(src_ref, tr_ref, h_hbm, o_ref, buf, sem, *, tile):
    i = pl.program_id(0)
    base = i * GATHER_ROWS
    per_tile = tile // GATHER_ROWS
    used = (i % per_tile) * GATHER_ROWS < tr_ref[i // per_tile]

    @pl.when(used)
    def _():
        def issue(r, carry):
            _row_copy(h_hbm, src_ref[base + r], buf, r, sem).start()
            return carry

        def drain(r, carry):
            _row_copy(h_hbm, 0, buf, r, sem).wait()
            return carry

        lax.fori_loop(0, GATHER_ROWS, issue, 0)
        lax.fori_loop(0, GATHER_ROWS, drain, 0)
        o_ref[...] = buf[...].astype(o_ref.dtype)

    @pl.when(jnp.logical_not(used))
    def _():
        o_ref[...] = jnp.zeros(o_ref.shape, o_ref.dtype)


def gather_rows(h, src, tile_rows, tile):
    n = src.shape[0]
    D = h.shape[1]
    assert GATHER_ROWS == MOE_SUB_ROWS and tile % GATHER_ROWS == 0
    return pl.pallas_call(
        functools.partial(_gather_body, tile=tile),
        out_shape=jax.ShapeDtypeStruct((n, D), BF16),
        grid_spec=pltpu.PrefetchScalarGridSpec(
            num_scalar_prefetch=2,
            grid=(n // GATHER_ROWS,),
            in_specs=[pl.BlockSpec(memory_space=pl.ANY)],
            out_specs=pl.BlockSpec((GATHER_ROWS, D), lambda i, src, tr: (i, 0)),
            scratch_shapes=[pltpu.VMEM((GATHER_ROWS, D), F32), pltpu.SemaphoreType.DMA]),
        compiler_params=_params(("arbitrary",)),
        name="gather_rows",
    )(src, tile_rows, h)


def _combine_body(pos_ref, y_hbm, wt_ref, x_ref, g_ref, x2_ref, h_ref, buf, sem, *, bm, n_full, rem):
    i = pl.program_id(0)

    def do(rows):
        base = i * bm

        def issue(r, carry):
            for k in range(TOP_K):
                _row_copy(y_hbm, pos_ref[TOP_K * (base + r) + k], buf.at[k], r, sem).start()
            return carry

        def drain(r, carry):
            for k in range(TOP_K):
                _row_copy(y_hbm, 0, buf.at[k], r, sem).wait()
            return carry

        lax.fori_loop(0, rows, issue, 0)
        lax.fori_loop(0, rows, drain, 0)
        w = wt_ref[:rows]
        y = w[:, 0:1] * buf[0, :rows] + w[:, 1:2] * buf[1, :rows]
        x2 = x_ref[:rows] + y
        x2_ref[:rows] = x2
        h_ref[:rows] = _rms(x2, g_ref[...]).astype(h_ref.dtype)

    _row_branches(i, n_full, bm, rem, do)


def combine_rows(y_sorted, pos, wts, x, g3, layer, bm=256):
    T, D = x.shape
    n_full, rem = divmod(T, bm)
    row = lambda i, pos: (i, 0)
    return pl.pallas_call(
        functools.partial(_combine_body, bm=bm, n_full=n_full, rem=rem),
        out_shape=(jax.ShapeDtypeStruct((T, D), F32), jax.ShapeDtypeStruct((T, D), BF16)),
        grid_spec=pltpu.PrefetchScalarGridSpec(
            num_scalar_prefetch=1,
            grid=(pl.cdiv(T, bm),),
            in_specs=[pl.BlockSpec(memory_space=pl.ANY),
                      pl.BlockSpec((bm, TOP_K), row), pl.BlockSpec((bm, D), row),
                      pl.BlockSpec((None, 1, D), lambda i, pos: (layer, 0, 0))],
            out_specs=(pl.BlockSpec((bm, D), row), pl.BlockSpec((bm, D), row)),
            scratch_shapes=[pltpu.VMEM((TOP_K, bm, D), F32), pltpu.SemaphoreType.DMA]),
        compiler_params=_params(("arbitrary",)),
        name="combine_rows",
    )(pos, y_sorted, wts, x, g3)


def _combine_experts_body(y_ref, idx_ref, wt_ref, x_ref, g_ref, x2_ref, h_ref, *, n_experts):
    rows = x_ref.shape[0]
    idx = idx_ref[...]
    wt = wt_ref[...]
    acc = jnp.zeros(x_ref.shape, F32)
    for e in range(n_experts):
        ce = jnp.sum(jnp.where(idx == e, wt, 0.0), axis=1, keepdims=True)
        acc = acc + ce * y_ref[e * rows:(e + 1) * rows, :]
    x2 = x_ref[...] + acc
    x2_ref[...] = x2
    h_ref[...] = _rms(x2, g_ref[...])


def combine_experts(y, idx, wts, x, g3, layer):
    T, D = x.shape
    E = y.shape[0] // T
    full = lambda a: pl.BlockSpec(a.shape, lambda i: (0,) * a.ndim)
    return pl.pallas_call(
        functools.partial(_combine_experts_body, n_experts=E),
        out_shape=(jax.ShapeDtypeStruct((T, D), F32), jax.ShapeDtypeStruct((T, D), F32)),
        grid=(1,),
        in_specs=[full(y), full(idx), full(wts), full(x), pl.BlockSpec((None, 1, D), lambda i: (layer, 0, 0))],
        out_specs=(full(x), full(x)),
        compiler_params=_params(("arbitrary",)),
        name="combine_experts",
    )(y, idx, wts, x, g3)


def _forward(x_prompt, x_sample, p_prompt, p_sample, cache_k, cache_v, cache_logf, state_pool, state_conv,
             page_table, norm_mix, norm_ffn, norm_ple, w_in, b_forget, q_norm, k_norm, w_pool, pool_scale,
             w_attn_out, conv_w, conv_b, conv_ln_g, conv_ln_b, w_conv_out, w_out, dense_w_gate, dense_w_up,
             dense_w_down, router_w, moe_w_gate, moe_w_up, moe_w_down, w_ple, w_ple_gate):
    B, L, D = x_prompt.shape
    DB, S, _ = x_sample.shape
    depth = norm_mix.shape[0]
    H = b_forget.shape[1]
    A = H * HEAD_DIM
    CP = w_pool.shape[1] * w_pool.shape[2]
    CC = conv_w.shape[2]
    page = cache_k.shape[2]
    past_len = page_table.shape[1] * page
    P, NS = B * L, DB * S
    c_q, c_k, c_v, c_f = CP, CP + A, CP + 2 * A, CP + 3 * A
    c_glu = c_f + H
    assert c_f == 7 * IN_TILE and CP == IN_TILE and A == 2 * IN_TILE and P % ROW_TILE == 0

    vec3 = lambda a: a.reshape(a.shape[0], 1, a.shape[1])
    norm_mix3, norm_ffn3, norm_ple3 = vec3(norm_mix), vec3(norm_ffn), vec3(norm_ple)
    b_forget3, pool_scale3 = vec3(b_forget), vec3(pool_scale)
    conv_b3, ln_g3, ln_b3 = vec3(conv_b), vec3(conv_ln_g), vec3(conv_ln_b)
    w_f = w_in[:, :, c_f:c_glu]
    w_rest32 = w_in[:, :, c_glu:]
    w_rest = w_rest32.astype(BF16)
    cache_kt = cache_k.transpose(0, 1, 3, 4, 2)
    cache_vt = cache_v.transpose(0, 1, 3, 4, 2)
    cache_lft = cache_logf.transpose(0, 1, 3, 2)
    own_head = (jnp.arange(A)[None, :] // HEAD_DIM) == (jnp.arange(S * H)[:, None] % H)
    dense4 = [w.reshape(w.shape[0], 1, *w.shape[1:]) for w in (dense_w_gate, dense_w_up, dense_w_down)]
    zero_pool = jnp.zeros((B, POOL_STATE, CP), F32)
    zero_conv = jnp.zeros((B, CONV_STATE, CC), F32)

    n_dense_tiles = P // ROW_TILE
    dense_te = jnp.zeros((n_dense_tiles,), jnp.int32)
    dense_tr = jnp.full((n_dense_tiles,), ROW_TILE, jnp.int32)
    n_moe_tiles = (TOP_K * P) // ROW_TILE + N_EXPERTS
    one_te = jnp.zeros((1,), jnp.int32)
    one_tr = jnp.full((1,), NS, jnp.int32)
    expert_ids = jnp.arange(N_EXPERTS, dtype=jnp.int32)

    xp = x_prompt.reshape(P, D)
    xs = x_sample.reshape(NS, D)
    outs = [[] for _ in range(10)]
    for i in range(depth):
        j = i // 2

        h = rms_norm_rows(xp, norm_mix3, i)
        z1 = project_pool_qkv(h, w_in, i, q_norm, k_norm)
        lf = project_logf(h, w_f, i, b_forget3)
        u = project_glu(h, w_rest, i, CC)
        gates = project_gates(h, w_rest, i, 2 * CC, 3 * D)
        ya, pool_p = pool_mixer_rows(z1, B, L, 1, zero_pool, w_pool, pool_scale3, i, 0, D)
        yc, conv_p = conv_module_rows(u, B, L, 1, zero_conv, conv_w, conv_b3, ln_g3, ln_b3, w_conv_out, i)
        c = cumsum_rows(lf, B, L).reshape(B, L, H // 2, 2)
        o = prompt_attention(z1, c.transpose(0, 2, 3, 1), B, L, H, c_q, c_k, c_v)
        merged = attn_out_merge(o, w_attn_out, i, gates, ya, yc)
        xp = out_proj_residual(merged, w_out, i, xp)
        if i % 2 == 0:
            h2 = rms_norm_rows(xp, norm_ffn3, i)
            y = ffn_tiles(h2, dense_te, dense_tr, *dense4, j)
            xp, h3 = add_norm_rows(xp, y, norm_ple3, i)
        else:
            h2f, idx, wts = route_rows(xp, norm_ffn3, i, router_w, j)
            pos, src, te, tr = moe_dispatch(idx, ROW_TILE, n_moe_tiles)
            x_sorted = gather_rows(h2f, src, tr, ROW_TILE)
            y_sorted = ffn_tiles(x_sorted, te, tr, moe_w_gate, moe_w_up, moe_w_down, j)
            xp, h3 = combine_rows(y_sorted, pos, wts, xp, norm_ple3, i)
        xp = ple_residual(h3, w_ple_gate, p_prompt[i].reshape(P, -1), w_ple, i, xp)

        hs = rms_norm_rows(xs, norm_mix3, i, out_dtype=F32)
        z1s = project_pool_qkv(hs, w_in, i, q_norm, k_norm, precise=True)
        lfs = project_logf(hs, w_f, i, b_forget3, precise=True)
        us = project_glu(hs, w_rest32, i, CC, precise=True)
        gates_s = project_gates(hs, w_rest32, i, 2 * CC, 3 * D, precise=True)
        ya_s, pool_s = pool_mixer_rows(z1s, DB, S, DB, state_pool[i], w_pool, pool_scale3, i, past_len, D,
                                       precise=True)
        yc_s, conv_s = conv_module_rows(us, DB, S, DB, state_conv[i], conv_w, conv_b3, ln_g3, ln_b3,
                                        w_conv_out, i, precise=True)
        q_bd = jnp.where(own_head, jnp.tile(z1s[:, c_q:c_k].reshape(DB, S * H, HEAD_DIM), (1, 1, H)), 0.0)
        new_rows = lambda a: jnp.pad(a.reshape(DB, S, A), ((0, 0), (0, NEW_PAD - S), (0, 0)))
        lfn = jnp.pad(lfs.reshape(DB, S, H).transpose(0, 2, 1), ((0, 0), (0, 0), (0, NEW_PAD - S)))
        o_s = sample_attention(q_bd, new_rows(z1s[:, c_k:c_v]), new_rows(z1s[:, c_v:c_f]), lfn,
                               cache_kt, cache_vt, cache_lft, page_table, i)
        merged_s = attn_out_merge(o_s.reshape(NS, A), w_attn_out, i, gates_s, ya_s, yc_s, precise=True)
        xs = out_proj_residual(merged_s, w_out, i, xs, precise=True)
        if i % 2 == 0:
            h2s = rms_norm_rows(xs, norm_ffn3, i, out_dtype=F32)
            ys = ffn_tiles(h2s, one_te, one_tr, *dense4, j, precise=True)
            xs, h3s = add_norm_rows(xs, ys, norm_ple3, i, out_dtype=F32)
        else:
            h2s, idx_s, wts_s = route_rows(xs, norm_ffn3, i, router_w, j)
            hit = jnp.any(idx_s.reshape(-1)[None, :] == expert_ids[:, None], axis=1)
            ys = ffn_tiles(h2s, expert_ids, jnp.where(hit, NS, 0).astype(jnp.int32), moe_w_gate, moe_w_up,
                           moe_w_down, j, precise=True)
            xs, h3s = combine_experts(ys, idx_s, wts_s, xs, norm_ple3, i)
        xs = ple_residual(h3s, w_ple_gate, p_sample[i].reshape(NS, -1), w_ple, i, xs, precise=True)

        heads = lambda a, n, l: a.reshape(n, l, H, HEAD_DIM)
        for lst, val in zip(outs, (
                heads(z1[:, c_k:c_v], B, L), heads(z1[:, c_v:c_f], B, L), lf.reshape(B, L, H), pool_p, conv_p,
                heads(z1s[:, c_k:c_v], DB, S), heads(z1s[:, c_v:c_f], DB, S), lfs.reshape(DB, S, H), pool_s,
                conv_s)):
            lst.append(val)
    return (xp.reshape(B, L, D), xs.reshape(DB, S, D), *(jnp.stack(l) for l in outs))


def kernel(x_prompt, x_sample, p_prompt, p_sample, cache_k, cache_v, cache_logf, state_pool, state_conv, page_table, norm_mix, norm_ffn, norm_ple, w_in, b_forget, q_norm, k_norm, w_pool, pool_scale, w_attn_out, conv_w, conv_b, conv_ln_g, conv_ln_b, w_conv_out, w_out, dense_w_gate, dense_w_up, dense_w_down, router_w, moe_w_gate, moe_w_up, moe_w_down, w_ple, w_ple_gate):
    return _forward(x_prompt, x_sample, p_prompt, p_sample, cache_k, cache_v, cache_logf, state_pool, state_conv,
                    page_table, norm_mix, norm_ffn, norm_ple, w_in, b_forget, q_norm, k_norm, w_pool, pool_scale,
                    w_attn_out, conv_w, conv_b, conv_ln_g, conv_ln_b, w_conv_out, w_out, dense_w_gate,
                    dense_w_up, dense_w_down, router_w, moe_w_gate, moe_w_up, moe_w_down, w_ple, w_ple_gate)
```

```python
import functools

import jax
import jax.numpy as jnp
from jax import lax
from jax.experimental import pallas as pl
from jax.experimental.pallas import tpu as pltpu

F32 = jnp.float32
BF16 = jnp.bfloat16

HEAD_DIM = 64
POOL_WINDOWS = (2, 4, 8, 16)
POOL_GROUP = 128
POOL_STATE = max(POOL_WINDOWS) - 1
CONV_WIDTH = 31
CONV_STATE = CONV_WIDTH - 1
N_EXPERTS = 8
TOP_K = 2
RMS_EPS = 1e-6
LN_EPS = 1e-5
ATTN_SCALE = HEAD_DIM ** -0.5
LOG2_E = 1.4426950408889634
LANES = 128

V7X_VMEM_LIMIT_BYTES = 56 * 1024 * 1024
ROW_TILE = 1024
MOE_SUB_ROWS = 256
FULL_SUB_ROWS = 512


def _params(sem):
    return pltpu.CompilerParams(dimension_semantics=sem, vmem_limit_bytes=V7X_VMEM_LIMIT_BYTES)


def _row_branches(i, n_full, bm, rem, do):
    if rem == 0:
        do(bm)
        return

    @pl.when(i < n_full)
    def _():
        do(bm)

    @pl.when(i == n_full)
    def _():
        do(rem)


def _rms(x, g):
    return x * lax.rsqrt(jnp.mean(x * x, axis=-1, keepdims=True) + RMS_EPS) * g


_NN = (((1,), (0,)), ((), ()))
_NT = (((1,), (1,)), ((), ()))


def _split(x):
    hi = x.astype(BF16)
    return hi, (x - hi.astype(F32)).astype(BF16)


def _dot3_parts(xh, xl, wh, wl, dims=_NN):
    m = xh.shape[0]
    both = lax.dot_general(jnp.concatenate([xh, xl], axis=0), wh, dims, preferred_element_type=F32)
    return both[:m] + (both[m:] + lax.dot_general(xh, wl, dims, preferred_element_type=F32))


def _dot3(x, w, dims=_NN):
    return _dot3_parts(*_split(x), *_split(w), dims)


def _norm_body(x_ref, g_ref, h_ref, *, bm, n_full, rem):
    def do(rows):
        h_ref[:rows] = _rms(x_ref[:rows], g_ref[...]).astype(h_ref.dtype)

    _row_branches(pl.program_id(0), n_full, bm, rem, do)


def rms_norm_rows(x, g3, layer, out_dtype=BF16, bm=512):
    T, D = x.shape
    bm = min(bm, T)
    n_full, rem = divmod(T, bm)
    return pl.pallas_call(
        functools.partial(_norm_body, bm=bm, n_full=n_full, rem=rem),
        out_shape=jax.ShapeDtypeStruct((T, D), out_dtype),
        grid=(pl.cdiv(T, bm),),
        in_specs=[pl.BlockSpec((bm, D), lambda i: (i, 0)),
                  pl.BlockSpec((None, 1, D), lambda i: (layer, 0, 0))],
        out_specs=pl.BlockSpec((bm, D), lambda i: (i, 0)),
        compiler_params=_params(("arbitrary",)),
        name="rms_norm_rows",
    )(x, g3)


def _add_norm_body(x_ref, y_ref, g_ref, x2_ref, h_ref, *, bm, n_full, rem):
    def do(rows):
        x2 = x_ref[:rows] + y_ref[:rows]
        x2_ref[:rows] = x2
        h_ref[:rows] = _rms(x2, g_ref[...]).astype(h_ref.dtype)

    _row_branches(pl.program_id(0), n_full, bm, rem, do)


def add_norm_rows(x, y, g3, layer, out_dtype=BF16, bm=512):
    T, D = x.shape
    bm = min(bm, T)
    n_full, rem = divmod(T, bm)
    spec = pl.BlockSpec((bm, D), lambda i: (i, 0))
    return pl.pallas_call(
        functools.partial(_add_norm_body, bm=bm, n_full=n_full, rem=rem),
        out_shape=(jax.ShapeDtypeStruct((T, D), F32), jax.ShapeDtypeStruct((T, D), out_dtype)),
        grid=(pl.cdiv(T, bm),),
        in_specs=[spec, spec, pl.BlockSpec((None, 1, D), lambda i: (layer, 0, 0))],
        out_specs=(spec, spec),
        compiler_params=_params(("arbitrary",)),
        name="add_norm_rows",
    )(x, y, g3)


def _mm_body(*refs, n_pairs, n_extra, n_out, cast_w, precise, epilogue, bm, n_full, rem):
    xs = refs[:n_pairs]
    ws = refs[n_pairs:2 * n_pairs]
    extras = refs[2 * n_pairs:2 * n_pairs + n_extra]
    outs = refs[2 * n_pairs + n_extra:2 * n_pairs + n_extra + n_out]
    wbs = refs[2 * n_pairs + n_extra + n_out:]
    j = pl.program_id(0)
    i = pl.program_id(1)

    @pl.when(i == 0)
    def _():
        k = 0
        for p in range(n_pairs):
            if cast_w[p]:
                wbs[k][...] = ws[p][...].astype(BF16)
                k += 1

    def do(rows):
        accs = []
        k = 0
        for p in range(n_pairs):
            x = xs[p][:rows]
            if precise:
                accs.append(_dot3(x, ws[p][...]))
                continue
            if x.dtype != BF16:
                x = x.astype(BF16)
            if cast_w[p]:
                w = wbs[k][...]
                k += 1
            else:
                w = ws[p][...]
            accs.append(jnp.dot(x, w, preferred_element_type=F32))
        epilogue(accs, rows, j, extras, outs)

    _row_branches(i, n_full, bm, rem, do)


def matmul_rows(pairs, extras, outs, epilogue, *, tn, n_tiles, name, bm, precise=False):
    T = pairs[0][0].shape[0]
    n_full, rem = divmod(T, bm)
    in_specs, args, cast_w, scratch = [], [], [], []
    for x, _, _, _ in pairs:
        in_specs.append(pl.BlockSpec((bm, x.shape[1]), lambda j, i: (i, 0)))
        args.append(x)
    for x, w, lead, off in pairs:
        K = x.shape[1]
        blk = (None,) * len(lead) + (K, tn)
        in_specs.append(pl.BlockSpec(blk, lambda j, i, lead=lead, off=off: (*lead, 0, j + off)))
        args.append(w)
        assert not precise or (w.dtype == F32 and x.dtype == F32)
        cast_w.append(w.dtype != BF16 and not precise)
        if cast_w[-1]:
            scratch.append(pltpu.VMEM((K, tn), BF16))
    for a, blk, imap in extras:
        in_specs.append(pl.BlockSpec(blk, imap))
        args.append(a)
    return pl.pallas_call(
        functools.partial(_mm_body, n_pairs=len(pairs), n_extra=len(extras), n_out=len(outs),
                          cast_w=tuple(cast_w), precise=precise, epilogue=epilogue, bm=bm, n_full=n_full,
                          rem=rem),
        out_shape=tuple(o[0] for o in outs),
        grid=(n_tiles, pl.cdiv(T, bm)),
        in_specs=in_specs,
        out_specs=tuple(pl.BlockSpec(o[1], o[2]) for o in outs),
        scratch_shapes=scratch,
        compiler_params=_params(("arbitrary", "arbitrary")),
        name=name,
    )(*args)


def _tile_spec(bm, tn, off=0):
    return (bm, tn), (lambda j, i, off=off: (i, j + off))


IN_TILE = 512


def _epi_pool_qkv(accs, rows, j, extras, outs):
    (acc,) = accs
    bd_ref, g_ref = extras
    (o_ref,) = outs
    is_head_norm = (j >= 1) & (j <= 4)

    @pl.when(is_head_norm)
    def _():
        hi, lo = _split(acc * acc)
        ss = (jnp.dot(hi, bd_ref[...], preferred_element_type=F32)
              + jnp.dot(lo, bd_ref[...], preferred_element_type=F32))
        o_ref[:rows] = acc * lax.rsqrt(ss * (1.0 / HEAD_DIM) + RMS_EPS) * g_ref[...]

    @pl.when(jnp.logical_not(is_head_norm))
    def _():
        o_ref[:rows] = acc


def project_pool_qkv(h, w_in, layer, q_norm, k_norm, precise=False):
    T = h.shape[0]
    tn = IN_TILE
    bm = min(ROW_TILE, T)
    n_tiles = 7
    heads_per_tile = tn // HEAD_DIM
    ones = jnp.ones((tn,), F32)
    gq = jnp.tile(q_norm[layer], heads_per_tile)
    gk = jnp.tile(k_norm[layer], heads_per_tile)
    gains = jnp.stack([ones, gq, gq, gk, gk, ones, ones]).reshape(n_tiles, 1, tn)
    grp = jnp.arange(tn) // HEAD_DIM
    bd = (grp[:, None] == grp[None, :]).astype(BF16)
    blk, imap = _tile_spec(bm, tn)
    (z1,) = matmul_rows(
        [(h, w_in, (layer,), 0)],
        [(bd, (tn, tn), lambda j, i: (0, 0)), (gains, (None, 1, tn), lambda j, i: (j, 0, 0))],
        [(jax.ShapeDtypeStruct((T, n_tiles * tn), F32), blk, imap)],
        _epi_pool_qkv, tn=tn, n_tiles=n_tiles, name="project_pool_qkv", bm=bm, precise=precise)
    return z1


def _epi_logf(accs, rows, j, extras, outs):
    (acc,) = accs
    (b_ref,) = extras
    (o_ref,) = outs
    o_ref[:rows] = jax.nn.log_sigmoid(acc + b_ref[...])


def project_logf(h, w_f, layer, b_forget3, precise=False):
    T = h.shape[0]
    H = w_f.shape[-1]
    bm = min(ROW_TILE, T)
    (lf,) = matmul_rows(
        [(h, w_f, (layer,), 0)],
        [(b_forget3, (None, 1, H), lambda j, i: (layer, 0, 0))],
        [(jax.ShapeDtypeStruct((T, H), F32), (bm, H), lambda j, i: (i, 0))],
        _epi_logf, tn=H, n_tiles=1, name="project_logf", bm=bm, precise=precise)
    return lf


def _epi_glu(accs, rows, j, extras, outs):
    a, b = accs
    (o_ref,) = outs
    o_ref[:rows] = a * jax.nn.sigmoid(b)


def project_glu(h, w_rest, layer, conv_dim, precise=False):
    T = h.shape[0]
    bm = min(ROW_TILE, T)
    (u,) = matmul_rows(
        [(h, w_rest, (layer,), 0), (h, w_rest, (layer,), 1)],
        [],
        [(jax.ShapeDtypeStruct((T, conv_dim), F32), (bm, conv_dim), lambda j, i: (i, 0))],
        _epi_glu, tn=conv_dim, n_tiles=1, name="project_glu", bm=bm, precise=precise)
    return u


def _epi_sigmoid(accs, rows, j, extras, outs):
    (acc,) = accs
    (o_ref,) = outs
    o_ref[:rows] = jax.nn.sigmoid(acc)


def project_gates(h, w_rest, layer, col0, n_cols, tn=1024, precise=False):
    T = h.shape[0]
    bm = min(ROW_TILE, T)
    blk, imap = _tile_spec(bm, tn)
    (g,) = matmul_rows(
        [(h, w_rest, (layer,), col0 // tn)],
        [],
        [(jax.ShapeDtypeStruct((T, n_cols), F32), blk, imap)],
        _epi_sigmoid, tn=tn, n_tiles=n_cols // tn, name="project_gates", bm=bm, precise=precise)
    return g


def _epi_merge(accs, rows, j, extras, outs):
    (yb,) = accs
    g0, g1, g2, ya, yc = extras
    (o_ref,) = outs
    merged = g0[:rows] * ya[:rows] + g1[:rows] * yb + g2[:rows] * yc[:rows]
    o_ref[:rows] = merged.astype(o_ref.dtype)


def attn_out_merge(o, w_attn_out, layer, gates, ya, yc, tn=1024, precise=False):
    T = o.shape[0]
    D = w_attn_out.shape[-1]
    nt = D // tn
    bm = min(ROW_TILE // 2, T)
    blk, imap = _tile_spec(bm, tn)
    (m,) = matmul_rows(
        [(o, w_attn_out, (layer,), 0)],
        [(gates, *_tile_spec(bm, tn, 0)), (gates, *_tile_spec(bm, tn, nt)),
         (gates, *_tile_spec(bm, tn, 2 * nt)), (ya, blk, imap), (yc, blk, imap)],
        [(jax.ShapeDtypeStruct((T, D), F32 if precise else BF16), blk, imap)],
        _epi_merge, tn=tn, n_tiles=nt, name="attn_out_merge", bm=bm, precise=precise)
    return m


def _epi_residual(accs, rows, j, extras, outs):
    (acc,) = accs
    (x_ref,) = extras
    (o_ref,) = outs
    o_ref[:rows] = x_ref[:rows] + acc


def out_proj_residual(merged, w_out, layer, x, tn=1024, precise=False):
    T, D = x.shape
    bm = min(ROW_TILE, T)
    blk, imap = _tile_spec(bm, tn)
    (x2,) = matmul_rows(
        [(merged, w_out, (layer,), 0)],
        [(x, blk, imap)],
        [(jax.ShapeDtypeStruct((T, D), F32), blk, imap)],
        _epi_residual, tn=tn, n_tiles=D // tn, name="out_proj_residual", bm=bm, precise=precise)
    return x2


def _epi_ple(accs, rows, j, extras, outs):
    gate_logit, emb = accs
    (x_ref,) = extras
    (o_ref,) = outs
    o_ref[:rows] = x_ref[:rows] + jax.nn.sigmoid(gate_logit) * emb


def ple_residual(h, w_ple_gate, p, w_ple, layer, x, tn=1024, precise=False):
    T, D = x.shape
    bm = min(ROW_TILE // 2, T)
    blk, imap = _tile_spec(bm, tn)
    (x2,) = matmul_rows(
        [(h, w_ple_gate, (layer,), 0), (p, w_ple, (layer,), 0)],
        [(x, blk, imap)],
        [(jax.ShapeDtypeStruct((T, D), F32), blk, imap)],
        _epi_ple, tn=tn, n_tiles=D // tn, name="ple_residual", bm=bm, precise=precise)
    return x2


POOL_PAD = 16
CONV_PAD = 32


def _pool_body(u_ref, pre_ref, w_ref, sc_ref, ya_ref, st_ref, ext_ref, *, nb, L, R, offset, precise):
    C = ext_ref.shape[-1]
    c = pl.program_id(1)

    @pl.when(c == 0)
    def _():
        for lb in range(nb):
            ext_ref[lb, 0:POOL_PAD - POOL_STATE, :] = jnp.zeros((POOL_PAD - POOL_STATE, C), F32)
            ext_ref[lb, POOL_PAD - POOL_STATE:POOL_PAD, :] = pre_ref[lb]
            ext_ref[lb, POOL_PAD:POOL_PAD + L, :] = u_ref[lb * L:(lb + 1) * L, :]
            st_ref[lb] = ext_ref[lb, POOL_PAD + L - POOL_STATE:POOL_PAD + L, :]

    t0 = 0 if L == R else pl.multiple_of(c * R, R)
    pos = offset + t0 + lax.broadcasted_iota(jnp.int32, (R, 1), 0)
    wins = [ext_ref[lb, pl.ds(t0, R + POOL_PAD), :] for lb in range(nb)]
    for g, w in enumerate(POOL_WINDOWS):
        cnt = jnp.minimum(pos + 1, w).astype(F32)
        rs = []
        for lb in range(nb):
            wg = wins[lb][:, g * POOL_GROUP:(g + 1) * POOL_GROUP]
            cur = wg[POOL_PAD:POOL_PAD + R]
            acc = cur
            for i in range(1, w):
                acc = acc + wg[POOL_PAD - i:POOL_PAD - i + R]
            rs.append(acc / cnt - cur)
        r = rs[0] if nb == 1 else jnp.concatenate(rs, axis=0)
        if precise:
            y = _dot3(r, w_ref[g])
        else:
            y = jnp.dot(r.astype(BF16), w_ref[g].astype(BF16), preferred_element_type=F32)
        og = y.shape[1]
        ya_ref[:, g * og:(g + 1) * og] = y * sc_ref[:, g * og:(g + 1) * og]


def pool_mixer_rows(z1, nseq, L, nb, prefix, w_pool, pool_scale3, layer, offset, d_model, precise=False):
    C = len(POOL_WINDOWS) * POOL_GROUP
    R = min(L, 512)
    n_chunks = L // R
    return pl.pallas_call(
        functools.partial(_pool_body, nb=nb, L=L, R=R, offset=offset, precise=precise),
        out_shape=(jax.ShapeDtypeStruct((nseq * L, d_model), F32),
                   jax.ShapeDtypeStruct((nseq, POOL_STATE, C), F32)),
        grid=(nseq // nb, n_chunks),
        in_specs=[pl.BlockSpec((nb * L, C), lambda s, c: (s, 0)),
                  pl.BlockSpec((nb, POOL_STATE, C), lambda s, c: (s, 0, 0)),
                  pl.BlockSpec((None,) + w_pool.shape[1:], lambda s, c: (layer, 0, 0, 0)),
                  pl.BlockSpec((None, 1, d_model), lambda s, c: (layer, 0, 0))],
        out_specs=(pl.BlockSpec((nb * R, d_model), lambda s, c: (s * n_chunks + c, 0)),
                   pl.BlockSpec((nb, POOL_STATE, C), lambda s, c: (s, 0, 0))),
        scratch_shapes=[pltpu.VMEM((nb, POOL_PAD + L, C), F32)],
        compiler_params=_params(("arbitrary", "arbitrary")),
        name="pool_mixer_rows",
    )(z1, prefix, w_pool, pool_scale3)


def _conv_body(u_ref, pre_ref, cw_ref, cb_ref, lg_ref, lb_ref, wo_ref, yc_ref, st_ref, ext_ref, *, nb, L, R,
               precise):
    C = ext_ref.shape[-1]
    c = pl.program_id(1)

    @pl.when(c == 0)
    def _():
        for lb in range(nb):
            ext_ref[lb, 0:CONV_PAD - CONV_STATE, :] = jnp.zeros((CONV_PAD - CONV_STATE, C), F32)
            ext_ref[lb, CONV_PAD - CONV_STATE:CONV_PAD, :] = pre_ref[lb]
            ext_ref[lb, CONV_PAD:CONV_PAD + L, :] = u_ref[lb * L:(lb + 1) * L, :]
            st_ref[lb] = ext_ref[lb, CONV_PAD + L - CONV_STATE:CONV_PAD + L, :]

    t0 = 0 if L == R else pl.multiple_of(c * R, R)
    ys = []
    for lb in range(nb):
        win = ext_ref[lb, pl.ds(t0, R + CONV_PAD), :]
        acc = jnp.zeros((R, C), F32)
        for j in range(CONV_WIDTH):
            s0 = CONV_PAD - CONV_STATE + j
            acc = acc + win[s0:s0 + R] * cw_ref[j:j + 1, :]
        ys.append(acc)
    y = (ys[0] if nb == 1 else jnp.concatenate(ys, axis=0)) + cb_ref[...]
    mu = jnp.mean(y, axis=-1, keepdims=True)
    yc = y - mu
    yn = yc * lax.rsqrt(jnp.mean(yc * yc, axis=-1, keepdims=True) + LN_EPS) * lg_ref[...] + lb_ref[...]
    act = yn * jax.nn.sigmoid(yn)
    if precise:
        yc_ref[...] = _dot3(act, wo_ref[...])
    else:
        yc_ref[...] = jnp.dot(act.astype(BF16), wo_ref[...].astype(BF16), preferred_element_type=F32)


def conv_module_rows(u, nseq, L, nb, prefix, conv_w, conv_b3, ln_g3, ln_b3, w_conv_out, layer, precise=False):
    C = u.shape[1]
    D = w_conv_out.shape[-1]
    R = min(L, 256)
    n_chunks = L // R
    vec = lambda a: pl.BlockSpec((None, 1, a.shape[-1]), lambda s, c: (layer, 0, 0))
    return pl.pallas_call(
        functools.partial(_conv_body, nb=nb, L=L, R=R, precise=precise),
        out_shape=(jax.ShapeDtypeStruct((nseq * L, D), F32), jax.ShapeDtypeStruct((nseq, CONV_STATE, C), F32)),
        grid=(nseq // nb, n_chunks),
        in_specs=[pl.BlockSpec((nb * L, C), lambda s, c: (s, 0)),
                  pl.BlockSpec((nb, CONV_STATE, C), lambda s, c: (s, 0, 0)),
                  pl.BlockSpec((None, CONV_WIDTH, C), lambda s, c: (layer, 0, 0)),
                  vec(conv_b3), vec(ln_g3), vec(ln_b3),
                  pl.BlockSpec((None, C, D), lambda s, c: (layer, 0, 0))],
        out_specs=(pl.BlockSpec((nb * R, D), lambda s, c: (s * n_chunks + c, 0)),
                   pl.BlockSpec((nb, CONV_STATE, C), lambda s, c: (s, 0, 0))),
        scratch_shapes=[pltpu.VMEM((nb, CONV_PAD + L, C), F32)],
        compiler_params=_params(("arbitrary", "arbitrary")),
        name="conv_module_rows",
    )(u, prefix, conv_w, conv_b3, ln_g3, ln_b3, w_conv_out)


CUMSUM_CHUNK = 128


def _cumsum_body(x_ref, o_ref, *, L):
    n = CUMSUM_CHUNK
    tri = (lax.broadcasted_iota(jnp.int32, (n, n), 0) >= lax.broadcasted_iota(jnp.int32, (n, n), 1)).astype(F32)
    carry = jnp.zeros((1, x_ref.shape[-1]), F32)
    for k in range(L // n):
        ck = jnp.dot(tri, x_ref[k * n:(k + 1) * n, :], precision=lax.Precision.HIGHEST,
                     preferred_element_type=F32) + carry
        o_ref[k * n:(k + 1) * n, :] = ck
        carry = ck[n - 1:n, :]


def cumsum_rows(logf, nseq, L):
    H = logf.shape[1]
    return pl.pallas_call(
        functools.partial(_cumsum_body, L=L),
        out_shape=jax.ShapeDtypeStruct((nseq * L, H), F32),
        grid=(nseq,),
        in_specs=[pl.BlockSpec((L, H), lambda b: (b, 0))],
        out_specs=pl.BlockSpec((L, H), lambda b: (b, 0)),
        compiler_params=_params(("arbitrary",)),
        name="cumsum_rows",
    )(logf)


KEY_CHUNKS_PER_UPDATE = 4
HEAD_PAIRS_PER_STEP = 2


def _flash_body(q_ref, k_ref, v_ref, ck_ref, o_ref, m_ref, l_ref, acc_ref, *, blk, nq):
    qi = pl.program_id(2)
    pair = 2 * HEAD_DIM
    n_heads = 2 * HEAD_PAIRS_PER_STEP
    cols = lambda h: slice((h // 2) * pair, (h // 2 + 1) * pair)
    q2 = q_ref[...] * (ATTN_SCALE * LOG2_E)
    first = lax.broadcasted_iota(jnp.int32, (1, pair), 1) < HEAD_DIM
    qh = [jnp.where(first if h % 2 == 0 else jnp.logical_not(first), q2[:, cols(h)], 0.0).astype(BF16)
          for h in range(n_heads)]
    m_ref[...] = jnp.full(m_ref.shape, -jnp.inf, F32)
    l_ref[...] = jnp.zeros(l_ref.shape, F32)
    acc_ref[...] = jnp.zeros(acc_ref.shape, F32)
    causal = (lax.broadcasted_iota(jnp.int32, (blk, blk), 0) >= lax.broadcasted_iota(jnp.int32, (blk, blk), 1))

    def update(chunks):
        ks, vs, cks = [], [], []
        for kc, _ in chunks:
            k0 = kc * blk
            ks.append(k_ref[k0:k0 + blk, :].astype(BF16))
            vs.append(v_ref[k0:k0 + blk, :].astype(BF16))
            cks.append(ck_ref[:, :, k0:k0 + blk] * LOG2_E)
        for h in range(n_heads):
            ss = []
            for (_, masked), k2, ck in zip(chunks, ks, cks):
                s = (lax.dot_general(qh[h], k2[:, cols(h)], _NT, preferred_element_type=F32)
                     - ck[h // 2, h % 2:h % 2 + 1, :])
                ss.append(jnp.where(causal, s, -jnp.inf) if masked else s)
            m_prev = m_ref[h]
            m_new = m_prev
            for s in ss:
                m_new = jnp.maximum(m_new, jnp.max(s, axis=1, keepdims=True))
            alpha = jnp.exp2(m_prev - m_new)
            l_new = alpha * l_ref[h]
            acc = alpha * acc_ref[h]
            for s, v2 in zip(ss, vs):
                p = jnp.exp2(s - m_new)
                l_new = l_new + jnp.sum(p, axis=1, keepdims=True)
                acc = acc + jnp.dot(p.astype(BF16), v2[:, cols(h)], preferred_element_type=F32)
            l_ref[h] = l_new
            acc_ref[h] = acc
            m_ref[h] = m_new

    for n in range(nq):
        @pl.when(qi == n)
        def _(n=n):
            chunks = [(c, c == n) for c in range(n + 1)]
            for g0 in range(0, n + 1, KEY_CHUNKS_PER_UPDATE):
                update(chunks[g0:g0 + KEY_CHUNKS_PER_UPDATE])

    outs = [jnp.where(first, acc_ref[h] / l_ref[h], acc_ref[h + 1] / l_ref[h + 1]) for h in range(0, n_heads, 2)]
    o_ref[...] = jnp.concatenate(outs, axis=1).astype(o_ref.dtype)


def prompt_attention(z1, ct_pairs, nseq, L, n_heads, q_col, k_col, v_col):
    blk = min(L, 512)
    nq = L // blk
    pp = HEAD_PAIRS_PER_STEP
    lanes = pp * 2 * HEAD_DIM
    qc, kc, vc = q_col // lanes, k_col // lanes, v_col // lanes
    return pl.pallas_call(
        functools.partial(_flash_body, blk=blk, nq=nq),
        out_shape=jax.ShapeDtypeStruct((nseq * L, n_heads * HEAD_DIM), BF16),
        grid=(nseq, n_heads // (2 * pp), nq),
        in_specs=[pl.BlockSpec((blk, lanes), lambda b, hp, qi: (b * nq + qi, qc + hp)),
                  pl.BlockSpec((L, lanes), lambda b, hp, qi: (b, kc + hp)),
                  pl.BlockSpec((L, lanes), lambda b, hp, qi: (b, vc + hp)),
                  pl.BlockSpec((None, pp, 2, L), lambda b, hp, qi: (b, hp, 0, 0))],
        out_specs=pl.BlockSpec((blk, lanes), lambda b, hp, qi: (b * nq + qi, hp)),
        scratch_shapes=[pltpu.VMEM((2 * pp, blk, 1), F32), pltpu.VMEM((2 * pp, blk, 1), F32),
                        pltpu.VMEM((2 * pp, blk, 2 * HEAD_DIM), F32)],
        compiler_params=_params(("arbitrary", "arbitrary", "arbitrary")),
        name="prompt_attention",
    )(z1, z1, z1, ct_pairs)


DECODE_PAGES_PER_STEP = 8
NEW_PAD = 16


def _decode_body(pt_ref, q_ref, kn_ref, vn_ref, lfn_ref, *rest, PG, H, page):
    k_refs, v_refs, lf_refs = rest[:PG], rest[PG:2 * PG], rest[2 * PG:3 * PG]
    o_ref, m_ref, l_ref, acc_ref, carry_ref, cq_ref = rest[3 * PG:]
    s = pl.program_id(1)
    TH, HD = q_ref.shape
    S = TH // H
    sh_h = H.bit_length() - 1
    qh, ql = _split(q_ref[...] * ATTN_SCALE)
    row = lax.broadcasted_iota(jnp.int32, (TH, 1), 0)
    rows_of = lambda a: jnp.concatenate([a] * S, axis=0)

    @pl.when(s == 0)
    def _():
        upper = (lax.broadcasted_iota(jnp.int32, (NEW_PAD, NEW_PAD), 0)
                 <= lax.broadcasted_iota(jnp.int32, (NEW_PAD, NEW_PAD), 1)).astype(F32)
        cn = rows_of(jnp.dot(lfn_ref[...], upper, precision=lax.Precision.HIGHEST,
                             preferred_element_type=F32))
        jj = lax.broadcasted_iota(jnp.int32, (TH, NEW_PAD), 1)
        tt = row >> sh_h
        cq = jnp.sum(jnp.where(jj == tt, cn, 0.0), axis=1, keepdims=True)
        cq_ref[...] = cq
        sc = _dot3_parts(qh, ql, *_split(kn_ref[...]), _NT) + (cq - cn)
        sc = jnp.where(jj <= tt, sc, -jnp.inf)
        m = jnp.max(sc, axis=1, keepdims=True)
        p = jnp.exp(sc - m)
        m_ref[...] = m
        l_ref[...] = jnp.sum(p, axis=1, keepdims=True)
        acc_ref[...] = _dot3(p, vn_ref[...])
        carry_ref[...] = jnp.zeros(carry_ref.shape, F32)

    lane = lax.broadcasted_iota(jnp.int32, (1, page), 1)
    carry = carry_ref[...]
    biases = []
    for r in range(PG):
        lf = lf_refs[r][...]
        after = jnp.where(lane < page - 1, pltpu.roll(lf, page - 1, axis=1), 0.0)
        sh = 1
        while sh < page:
            after = after + jnp.where(lane < page - sh, pltpu.roll(after, page - sh, axis=1), 0.0)
            sh *= 2
        biases.append(rows_of(after) + carry)
        carry = carry + rows_of(jnp.sum(lf, axis=1, keepdims=True))
    carry_ref[...] = carry

    kT = jnp.concatenate([k_refs[r][...].reshape(HD, page) for r in range(PG)], axis=1)
    sc = _dot3_parts(qh, ql, *_split(kT)) + (jnp.concatenate(biases, axis=1) + cq_ref[...])
    m_prev = m_ref[...]
    m_new = jnp.maximum(m_prev, jnp.max(sc, axis=1, keepdims=True))
    alpha = jnp.exp(m_prev - m_new)
    p = jnp.exp(sc - m_new)
    l_new = alpha * l_ref[...] + jnp.sum(p, axis=1, keepdims=True)
    vT = jnp.concatenate([v_refs[r][...].reshape(HD, page) for r in range(PG)], axis=1)
    acc = alpha * acc_ref[...] + _dot3(p, vT, _NT)
    m_ref[...] = m_new
    l_ref[...] = l_new
    acc_ref[...] = acc

    @pl.when(s == pl.num_programs(1) - 1)
    def _():
        own_head = (lax.broadcasted_iota(jnp.int32, (TH, HD), 1) >> (HEAD_DIM.bit_length() - 1)) == (row & (H - 1))
        od = jnp.where(own_head, acc / l_new, 0.0)
        o_ref[...] = jnp.concatenate([jnp.sum(od[t * H:(t + 1) * H], axis=0, keepdims=True) for t in range(S)],
                                     axis=0)


def sample_attention(q_bd, kn, vn, lfn, cache_kt, cache_vt, cache_lft, page_table, layer):
    B, TH, HD = q_bd.shape
    _, _, H, Dh, page = cache_kt.shape
    S = TH // H
    n_pages = page_table.shape[1]
    PG = DECODE_PAGES_PER_STEP
    assert n_pages % PG == 0 and H & (H - 1) == 0 and Dh == HEAD_DIM and page % LANES == 0 and S <= NEW_PAD

    def page_map(r):
        return lambda b, s, pt: (layer, pt[b, n_pages - 1 - (s * PG + r)], 0, 0, 0)

    def lf_map(r):
        return lambda b, s, pt: (layer, pt[b, n_pages - 1 - (s * PG + r)], 0, 0)

    per_batch = lambda a: pl.BlockSpec((None,) + a.shape[1:], lambda b, s, pt: (b, 0, 0))
    in_specs = [per_batch(q_bd), per_batch(kn), per_batch(vn), per_batch(lfn)]
    in_specs += [pl.BlockSpec((None, None, H, Dh, page), page_map(r)) for r in range(PG)]
    in_specs += [pl.BlockSpec((None, None, H, Dh, page), page_map(r)) for r in range(PG)]
    in_specs += [pl.BlockSpec((None, None, H, page), lf_map(r)) for r in range(PG)]
    return pl.pallas_call(
        functools.partial(_decode_body, PG=PG, H=H, page=page),
        out_shape=jax.ShapeDtypeStruct((B, S, HD), F32),
        grid_spec=pltpu.PrefetchScalarGridSpec(
            num_scalar_prefetch=1,
            grid=(B, n_pages // PG),
            in_specs=in_specs,
            out_specs=pl.BlockSpec((None, S, HD), lambda b, s, pt: (b, 0, 0)),
            scratch_shapes=[pltpu.VMEM((TH, 1), F32), pltpu.VMEM((TH, 1), F32), pltpu.VMEM((TH, HD), F32),
                            pltpu.VMEM((TH, 1), F32), pltpu.VMEM((TH, 1), F32)]),
        compiler_params=_params(("arbitrary", "arbitrary")),
        name="sample_attention",
    )(page_table, q_bd, kn, vn, lfn, *([cache_kt] * PG), *([cache_vt] * PG), *([cache_lft] * PG))


FFN_CHUNK = 512


def _ffn_body(te_ref, tr_ref, x_ref, wg_ref, wu_ref, wd_ref, o_ref, *scratch, bm, sub, full, precise):
    s = pl.program_id(0)
    f = pl.program_id(1)
    rows_valid = tr_ref[s]

    def accumulate(r0, n, y):
        @pl.when(f == 0)
        def _():
            o_ref[r0:r0 + n] = y

        @pl.when(f > 0)
        def _():
            o_ref[r0:r0 + n] += y

    if precise:
        @pl.when(rows_valid > 0)
        def _():
            x = x_ref[...]
            g = _dot3(x, wg_ref[...])
            u = _dot3(x, wu_ref[...])
            accumulate(0, bm, _dot3(g * jax.nn.sigmoid(g) * u, wd_ref[...]))
    else:
        wgb, wub, wdb = scratch

        def block(r0, n):
            x = x_ref[r0:r0 + n]
            g = jnp.dot(x, wgb[...], preferred_element_type=F32)
            u = jnp.dot(x, wub[...], preferred_element_type=F32)
            a = (g * jax.nn.sigmoid(g) * u).astype(BF16)
            accumulate(r0, n, jnp.dot(a, wdb[...], preferred_element_type=F32))

        def cast_weights():
            wgb[...] = wg_ref[...].astype(BF16)
            wub[...] = wu_ref[...].astype(BF16)
            wdb[...] = wd_ref[...].astype(BF16)

        @pl.when(rows_valid == bm)
        def _():
            cast_weights()
            for r0 in range(0, bm, full):
                block(r0, full)

        @pl.when((rows_valid > 0) & (rows_valid < bm))
        def _():
            cast_weights()
            block(0, sub)
            for sb in range(1, bm // sub):
                @pl.when(sb * sub < rows_valid)
                def _():
                    block(sb * sub, sub)

                @pl.when((sb * sub >= rows_valid) & (f == 0))
                def _():
                    o_ref[sb * sub:(sb + 1) * sub] = jnp.zeros((sub, o_ref.shape[1]), F32)

    @pl.when((rows_valid == 0) & (f == 0))
    def _():
        o_ref[...] = jnp.zeros(o_ref.shape, F32)


def ffn_tiles(x, tile_expert, tile_rows, w_gate, w_up, w_down, lw, bm=ROW_TILE, precise=False):
    D = x.shape[1]
    F = w_gate.shape[-1]
    tf = FFN_CHUNK
    nf = F // tf
    bm = min(bm, x.shape[0])
    n_tiles = tile_expert.shape[0]
    assert precise or x.shape[0] == n_tiles * bm

    def f_eff(s, f, tr):
        return jnp.where(tr[s] > 0, f, nf - 1)

    x_spec = (pl.BlockSpec((bm, D), lambda s, f, te, tr: (0, 0)) if precise else
              pl.BlockSpec((bm, D), lambda s, f, te, tr: (s, 0), pipeline_mode=pl.Buffered(1)))
    in_specs = [x_spec,
                pl.BlockSpec((None, None, D, tf), lambda s, f, te, tr: (lw, te[s], 0, f_eff(s, f, tr))),
                pl.BlockSpec((None, None, D, tf), lambda s, f, te, tr: (lw, te[s], 0, f_eff(s, f, tr))),
                pl.BlockSpec((None, None, tf, D), lambda s, f, te, tr: (lw, te[s], f_eff(s, f, tr), 0))]
    scratch = [] if precise else [pltpu.VMEM((D, tf), BF16), pltpu.VMEM((D, tf), BF16),
                                  pltpu.VMEM((tf, D), BF16)]
    return pl.pallas_call(
        functools.partial(_ffn_body, bm=bm, sub=min(MOE_SUB_ROWS, bm), full=min(FULL_SUB_ROWS, bm),
                          precise=precise),
        out_shape=jax.ShapeDtypeStruct((n_tiles * bm, D), F32),
        grid_spec=pltpu.PrefetchScalarGridSpec(
            num_scalar_prefetch=2,
            grid=(n_tiles, nf),
            in_specs=in_specs,
            out_specs=pl.BlockSpec((bm, D), lambda s, f, te, tr: (s, 0), pipeline_mode=pl.Buffered(1)),
            scratch_shapes=scratch),
        compiler_params=_params(("arbitrary", "arbitrary")),
        name="ffn_tiles",
    )(tile_expert, tile_rows, x, w_gate, w_up, w_down)


def _router_body(x_ref, g_ref, wr_ref, h_ref, idx_ref, wt_ref, *, bm, n_full, rem):
    def do(rows):
        h = _rms(x_ref[:rows], g_ref[...])
        h_ref[:rows] = h
        logits = jnp.dot(h, wr_ref[...], precision=lax.Precision.HIGHEST, preferred_element_type=F32)
        E = logits.shape[1]
        lane = lax.broadcasted_iota(jnp.int32, (rows, E), 1)
        m1 = jnp.max(logits, axis=1, keepdims=True)
        i1 = jnp.min(jnp.where(logits == m1, lane, E), axis=1, keepdims=True)
        rest = jnp.where(lane == i1, -jnp.inf, logits)
        m2 = jnp.max(rest, axis=1, keepdims=True)
        i2 = jnp.min(jnp.where(rest == m2, lane, E), axis=1, keepdims=True)
        e = jnp.exp(m2 - m1)
        w1 = 1.0 / (1.0 + e)
        slot = lax.broadcasted_iota(jnp.int32, (rows, TOP_K), 1)
        idx_ref[:rows] = jnp.where(slot == 0, i1, i2)
        wt_ref[:rows] = jnp.where(slot == 0, w1, e * w1)

    _row_branches(pl.program_id(0), n_full, bm, rem, do)


def route_rows(x, g3, layer, router_w, lw, bm=512):
    T, D = x.shape
    E = router_w.shape[-1]
    bm = min(bm, T)
    n_full, rem = divmod(T, bm)
    return pl.pallas_call(
        functools.partial(_router_body, bm=bm, n_full=n_full, rem=rem),
        out_shape=(jax.ShapeDtypeStruct((T, D), F32), jax.ShapeDtypeStruct((T, TOP_K), jnp.int32),
                   jax.ShapeDtypeStruct((T, TOP_K), F32)),
        grid=(pl.cdiv(T, bm),),
        in_specs=[pl.BlockSpec((bm, D), lambda i: (i, 0)),
                  pl.BlockSpec((None, 1, D), lambda i: (layer, 0, 0)),
                  pl.BlockSpec((None, D, E), lambda i: (lw, 0, 0))],
        out_specs=(pl.BlockSpec((bm, D), lambda i: (i, 0)), pl.BlockSpec((bm, TOP_K), lambda i: (i, 0)),
                   pl.BlockSpec((bm, TOP_K), lambda i: (i, 0))),
        compiler_params=_params(("arbitrary",)),
        name="route_rows",
    )(x, g3, router_w)


def moe_dispatch(idx, bm, n_tiles):
    T = idx.shape[0]
    e_flat = idx.reshape(-1)
    onehot = (e_flat[:, None] == jnp.arange(N_EXPERTS, dtype=jnp.int32)[None, :]).astype(jnp.int32)
    incl = jnp.cumsum(onehot, axis=0)
    counts = incl[-1]
    rank = jnp.sum((incl - onehot) * onehot, axis=1)
    tiles_e = (counts + bm - 1) // bm
    tile_end = jnp.cumsum(tiles_e)
    tile_start = tile_end - tiles_e
    pos = (tile_start[e_flat] * bm + rank).astype(jnp.int32)
    src = jnp.zeros((n_tiles * bm,), jnp.int32).at[pos].set(jnp.arange(TOP_K * T, dtype=jnp.int32) // TOP_K)
    tile_ids = jnp.arange(n_tiles, dtype=jnp.int32)
    used = tile_ids < tile_end[-1]
    owner = jnp.sum((tile_ids[:, None] >= tile_end[None, :]).astype(jnp.int32), axis=1)
    owner = jnp.where(used, owner, owner[jnp.maximum(tile_end[-1] - 1, 0)])
    rows = jnp.clip(counts[owner] - (tile_ids - tile_start[owner]) * bm, 0, bm)
    rows = jnp.where(used, rows, 0).astype(jnp.int32)
    return pos, src, owner.astype(jnp.int32), rows


GATHER_ROWS = 256


def _row_copy(src_hbm, src_row, dst_vmem, dst_row, sem):
    return pltpu.make_async_copy(src_hbm.at[pl.ds(src_row, 1)], dst_vmem.at[pl.ds(dst_row, 1)], sem)


def _gather_body(src_ref, tr_ref, h_hbm, o_ref, buf, sem, *, tile):
    i = pl.program_id(0)
    base = i * GATHER_ROWS
    per_tile = tile // GATHER_ROWS
    used = (i % per_tile) * GATHER_ROWS < tr_ref[i // per_tile]

    @pl.when(used)
    def _():
        def issue(r, carry):
            _row_copy(h_hbm, src_ref[base + r], buf, r, sem).start()
            return carry

        def drain(r, carry):
            _row_copy(h_hbm, 0, buf, r, sem).wait()
            return carry

        lax.fori_loop(0, GATHER_ROWS, issue, 0)
        lax.fori_loop(0, GATHER_ROWS, drain, 0)
        o_ref[...] = buf[...].astype(o_ref.dtype)

    @pl.when(jnp.logical_not(used))
    def _():
        o_ref[...] = jnp.zeros(o_ref.shape, o_ref.dtype)


def gather_rows(h, src, tile_rows, tile):
    n = src.shape[0]
    D = h.shape[1]
    assert GATHER_ROWS == MOE_SUB_ROWS and tile % GATHER_ROWS == 0
    return pl.pallas_call(
        functools.partial(_gather_body, tile=tile),
        out_shape=jax.ShapeDtypeStruct((n, D), BF16),
        grid_spec=pltpu.PrefetchScalarGridSpec(
            num_scalar_prefetch=2,
            grid=(n // GATHER_ROWS,),
            in_specs=[pl.BlockSpec(memory_space=pl.ANY)],
            out_specs=pl.BlockSpec((GATHER_ROWS, D), lambda i, src, tr: (i, 0)),
            scratch_shapes=[pltpu.VMEM((GATHER_ROWS, D), F32), pltpu.SemaphoreType.DMA]),
        compiler_params=_params(("arbitrary",)),
        name="gather_rows",
    )(src, tile_rows, h)


def _combine_body(pos_ref, y_hbm, wt_ref, x_ref, g_ref, x2_ref, h_ref, buf, sem, *, bm, n_full, rem):
    i = pl.program_id(0)

    def do(rows):
        base = i * bm

        def issue(r, carry):
            for k in range(TOP_K):
                _row_copy(y_hbm, pos_ref[TOP_K * (base + r) + k], buf.at[k], r, sem).start()
            return carry

        def drain(r, carry):
            for k in range(TOP_K):
                _row_copy(y_hbm, 0, buf.at[k], r, sem).wait()
            return carry

        lax.fori_loop(0, rows, issue, 0)
        lax.fori_loop(0, rows, drain, 0)
        w = wt_ref[:rows]
        y = w[:, 0:1] * buf[0, :rows] + w[:, 1:2] * buf[1, :rows]
        x2 = x_ref[:rows] + y
        x2_ref[:rows] = x2
        h_ref[:rows] = _rms(x2, g_ref[...]).astype(h_ref.dtype)

    _row_branches(i, n_full, bm, rem, do)


def combine_rows(y_sorted, pos, wts, x, g3, layer, bm=256):
    T, D = x.shape
    n_full, rem = divmod(T, bm)
    row = lambda i, pos: (i, 0)
    return pl.pallas_call(
        functools.partial(_combine_body, bm=bm, n_full=n_full, rem=rem),
        out_shape=(jax.ShapeDtypeStruct((T, D), F32), jax.ShapeDtypeStruct((T, D), BF16)),
        grid_spec=pltpu.PrefetchScalarGridSpec(
            num_scalar_prefetch=1,
            grid=(pl.cdiv(T, bm),),
            in_specs=[pl.BlockSpec(memory_space=pl.ANY),
                      pl.BlockSpec((bm, TOP_K), row), pl.BlockSpec((bm, D), row),
                      pl.BlockSpec((None, 1, D), lambda i, pos: (layer, 0, 0))],
            out_specs=(pl.BlockSpec((bm, D), row), pl.BlockSpec((bm, D), row)),
            scratch_shapes=[pltpu.VMEM((TOP_K, bm, D), F32), pltpu.SemaphoreType.DMA]),
        compiler_params=_params(("arbitrary",)),
        name="combine_rows",
    )(pos, y_sorted, wts, x, g3)


def _combine_experts_body(y_ref, idx_ref, wt_ref, x_ref, g_ref, x2_ref, h_ref, *, n_experts):
    rows = x_ref.shape[0]
    idx = idx_ref[...]
    wt = wt_ref[...]
    acc = jnp.zeros(x_ref.shape, F32)
    for e in range(n_experts):
        ce = jnp.sum(jnp.where(idx == e, wt, 0.0), axis=1, keepdims=True)
        acc = acc + ce * y_ref[e * rows:(e + 1) * rows, :]
    x2 = x_ref[...] + acc
    x2_ref[...] = x2
    h_ref[...] = _rms(x2, g_ref[...])


def combine_experts(y, idx, wts, x, g3, layer):
    T, D = x.shape
    E = y.shape[0] // T
    full = lambda a: pl.BlockSpec(a.shape, lambda i: (0,) * a.ndim)
    return pl.pallas_call(
        functools.partial(_combine_experts_body, n_experts=E),
        out_shape=(jax.ShapeDtypeStruct((T, D), F32), jax.ShapeDtypeStruct((T, D), F32)),
        grid=(1,),
        in_specs=[full(y), full(idx), full(wts), full(x), pl.BlockSpec((None, 1, D), lambda i: (layer, 0, 0))],
        out_specs=(full(x), full(x)),
        compiler_params=_params(("arbitrary",)),
        name="combine_experts",
    )(y, idx, wts, x, g3)


def _forward(x_prompt, x_sample, p_prompt, p_sample, cache_k, cache_v, cache_logf, state_pool, state_conv,
             page_table, norm_mix, norm_ffn, norm_ple, w_in, b_forget, q_norm, k_norm, w_pool, pool_scale,
             w_attn_out, conv_w, conv_b, conv_ln_g, conv_ln_b, w_conv_out, w_out, dense_w_gate, dense_w_up,
             dense_w_down, router_w, moe_w_gate, moe_w_up, moe_w_down, w_ple, w_ple_gate):
    B, L, D = x_prompt.shape
    DB, S, _ = x_sample.shape
    depth = norm_mix.shape[0]
    H = b_forget.shape[1]
    A = H * HEAD_DIM
    CP = w_pool.shape[1] * w_pool.shape[2]
    CC = conv_w.shape[2]
    page = cache_k.shape[2]
    past_len = page_table.shape[1] * page
    P, NS = B * L, DB * S
    c_q, c_k, c_v, c_f = CP, CP + A, CP + 2 * A, CP + 3 * A
    c_glu = c_f + H
    assert c_f == 7 * IN_TILE and CP == IN_TILE and A == 2 * IN_TILE and P % ROW_TILE == 0

    vec3 = lambda a: a.reshape(a.shape[0], 1, a.shape[1])
    norm_mix3, norm_ffn3, norm_ple3 = vec3(norm_mix), vec3(norm_ffn), vec3(norm_ple)
    b_forget3, pool_scale3 = vec3(b_forget), vec3(pool_scale)
    conv_b3, ln_g3, ln_b3 = vec3(conv_b), vec3(conv_ln_g), vec3(conv_ln_b)
    w_f = w_in[:, :, c_f:c_glu]
    w_rest32 = w_in[:, :, c_glu:]
    w_rest = w_rest32.astype(BF16)
    cache_kt = cache_k.transpose(0, 1, 3, 4, 2)
    cache_vt = cache_v.transpose(0, 1, 3, 4, 2)
    cache_lft = cache_logf.transpose(0, 1, 3, 2)
    own_head = (jnp.arange(A)[None, :] // HEAD_DIM) == (jnp.arange(S * H)[:, None] % H)
    dense4 = [w.reshape(w.shape[0], 1, *w.shape[1:]) for w in (dense_w_gate, dense_w_up, dense_w_down)]
    zero_pool = jnp.zeros((B, POOL_STATE, CP), F32)
    zero_conv = jnp.zeros((B, CONV_STATE, CC), F32)

    n_dense_tiles = P // ROW_TILE
    dense_te = jnp.zeros((n_dense_tiles,), jnp.int32)
    dense_tr = jnp.full((n_dense_tiles,), ROW_TILE, jnp.int32)
    n_moe_tiles = (TOP_K * P) // ROW_TILE + N_EXPERTS
    one_te = jnp.zeros((1,), jnp.int32)
    one_tr = jnp.full((1,), NS, jnp.int32)
    expert_ids = jnp.arange(N_EXPERTS, dtype=jnp.int32)

    xp = x_prompt.reshape(P, D)
    xs = x_sample.reshape(NS, D)
    outs = [[] for _ in range(10)]
    for i in range(depth):
        j = i // 2

        h = rms_norm_rows(xp, norm_mix3, i)
        z1 = project_pool_qkv(h, w_in, i, q_norm, k_norm)
        lf = project_logf(h, w_f, i, b_forget3)
        u = project_glu(h, w_rest, i, CC)
        gates = project_gates(h, w_rest, i, 2 * CC, 3 * D)
        ya, pool_p = pool_mixer_rows(z1, B, L, 1, zero_pool, w_pool, pool_scale3, i, 0, D)
        yc, conv_p = conv_module_rows(u, B, L, 1, zero_conv, conv_w, conv_b3, ln_g3, ln_b3, w_conv_out, i)
        c = cumsum_rows(lf, B, L).reshape(B, L, H // 2, 2)
        o = prompt_attention(z1, c.transpose(0, 2, 3, 1), B, L, H, c_q, c_k, c_v)
        merged = attn_out_merge(o, w_attn_out, i, gates, ya, yc)
        xp = out_proj_residual(merged, w_out, i, xp)
        if i % 2 == 0:
            h2 = rms_norm_rows(xp, norm_ffn3, i)
            y = ffn_tiles(h2, dense_te, dense_tr, *dense4, j)
            xp, h3 = add_norm_rows(xp, y, norm_ple3, i)
        else:
            h2f, idx, wts = route_rows(xp, norm_ffn3, i, router_w, j)
            pos, src, te, tr = moe_dispatch(idx, ROW_TILE, n_moe_tiles)
            x_sorted = gather_rows(h2f, src, tr, ROW_TILE)
            y_sorted = ffn_tiles(x_sorted, te, tr, moe_w_gate, moe_w_up, moe_w_down, j)
            xp, h3 = combine_rows(y_sorted, pos, wts, xp, norm_ple3, i)
        xp = ple_residual(h3, w_ple_gate, p_prompt[i].reshape(P, -1), w_ple, i, xp)

        hs = rms_norm_rows(xs, norm_mix3, i, out_dtype=F32)
        z1s = project_pool_qkv(hs, w_in, i, q_norm, k_norm, precise=True)
        lfs = project_logf(hs, w_f, i, b_forget3, precise=True)
        us = project_glu(hs, w_rest32, i, CC, precise=True)
        gates_s = project_gates(hs, w_rest32, i, 2 * CC, 3 * D, precise=True)
        ya_s, pool_s = pool_mixer_rows(z1s, DB, S, DB, state_pool[i], w_pool, pool_scale3, i, past_len, D,
                                       precise=True)
        yc_s, conv_s = conv_module_rows(us, DB, S, DB, state_conv[i], conv_w, conv_b3, ln_g3, ln_b3,
                                        w_conv_out, i, precise=True)
        q_bd = jnp.where(own_head, jnp.tile(z1s[:, c_q:c_k].reshape(DB, S * H, HEAD_DIM), (1, 1, H)), 0.0)
        new_rows = lambda a: jnp.pad(a.reshape(DB, S, A), ((0, 0), (0, NEW_PAD - S), (0, 0)))
        lfn = jnp.pad(lfs.reshape(DB, S, H).transpose(0, 2, 1), ((0, 0), (0, 0), (0, NEW_PAD - S)))
        o_s = sample_attention(q_bd, new_rows(z1s[:, c_k:c_v]), new_rows(z1s[:, c_v:c_f]), lfn,
                               cache_kt, cache_vt, cache_lft, page_table, i)
        merged_s = attn_out_merge(o_s.reshape(NS, A), w_attn_out, i, gates_s, ya_s, yc_s, precise=True)
        xs = out_proj_residual(merged_s, w_out, i, xs, precise=True)
        if i % 2 == 0:
            h2s = rms_norm_rows(xs, norm_ffn3, i, out_dtype=F32)
            ys = ffn_tiles(h2s, one_te, one_tr, *dense4, j, precise=True)
            xs, h3s = add_norm_rows(xs, ys, norm_ple3, i, out_dtype=F32)
        else:
            h2s, idx_s, wts_s = route_rows(xs, norm_ffn3, i, router_w, j)
            hit = jnp.any(idx_s.reshape(-1)[None, :] == expert_ids[:, None], axis=1)
            ys = ffn_tiles(h2s, expert_ids, jnp.where(hit, NS, 0).astype(jnp.int32), moe_w_gate, moe_w_up,
                           moe_w_down, j, precise=True)
            xs, h3s = combine_experts(ys, idx_s, wts_s, xs, norm_ple3, i)
        xs = ple_residual(h3s, w_ple_gate, p_sample[i].reshape(NS, -1), w_ple, i, xs, precise=True)

        heads = lambda a, n, l: a.reshape(n, l, H, HEAD_DIM)
        for lst, val in zip(outs, (
                heads(z1[:, c_k:c_v], B, L), heads(z1[:, c_v:c_f], B, L), lf.reshape(B, L, H), pool_p, conv_p,
                heads(z1s[:, c_k:c_v], DB, S), heads(z1s[:, c_v:c_f], DB, S), lfs.reshape(DB, S, H), pool_s,
                conv_s)):
            lst.append(val)
    return (xp.reshape(B, L, D), xs.reshape(DB, S, D), *(jnp.stack(l) for l in outs))


def kernel(x_prompt, x_sample, p_prompt, p_sample, cache_k, cache_v, cache_logf, state_pool, state_conv, page_table, norm_mix, norm_ffn, norm_ple, w_in, b_forget, q_norm, k_norm, w_pool, pool_scale, w_attn_out, conv_w, conv_b, conv_ln_g, conv_ln_b, w_conv_out, w_out, dense_w_gate, dense_w_up, dense_w_down, router_w, moe_w_gate, moe_w_up, moe_w_down, w_ple, w_ple_gate):
    return _forward(x_prompt, x_sample, p_prompt, p_sample, cache_k, cache_v, cache_logf, state_pool, state_conv,
                    page_table, norm_mix, norm_ffn, norm_ple, w_in, b_forget, q_norm, k_norm, w_pool, pool_scale,
                    w_attn_out, conv_w, conv_b, conv_ln_g, conv_ln_b, w_conv_out, w_out, dense_w_gate,
                    dense_w_up, dense_w_down, router_w, moe_w_gate, moe_w_up, moe_w_down, w_ple, w_ple_gate)
```
